```python
import jax
import jax.numpy as jnp
from jax import lax
import numpy as np

D_MODEL = 1024
BATCH = 8
SEQ = 2048
DEPTH = 1
DEC_BATCH = 128
DEC_SEQ = 4
PAST_LEN = 8192
PAGE_SIZE = 128

MLA_HEADS = 8
Q_LORA = 512
KV_LORA = 256
NOPE_DIM = 64
ROPE_DIM = 32
V_DIM = 64
MLA_QK = NOPE_DIM + ROPE_DIM
MLA_W = MLA_HEADS * V_DIM
MLA_SCALE = MLA_QK ** -0.5
MLA_THETA = 10000.0

MOBA_HEADS = 8
MOBA_HD = 64
MOBA_W = MOBA_HEADS * MOBA_HD
MOBA_ROT = MOBA_HD // 4
MOBA_BLOCK = 256
MOBA_TOPK = 3
MOBA_SCALE = MOBA_HD ** -0.5
ROPE_THETA = 500000.0

D_FF = 2816
CONV_W = 3

Q_BLOCK = 128
EPS = 1e-6

kernel_name = 'hybrid_mla_moba_convffn_step'


def rmsnorm(x, g):
    xf = x.astype(jnp.float32)
    y = xf * lax.rsqrt(jnp.mean(xf * xf, axis=-1, keepdims=True) + EPS)
    return (y * g.astype(jnp.float32)).astype(x.dtype)


def rope(x, pos, theta):
    d = x.shape[-1]
    half = d // 2
    inv = 1.0 / (theta ** (jnp.arange(half, dtype=jnp.float32) * (2.0 / d)))
    ang = pos.astype(jnp.float32)[:, None] * inv[None, :]
    cos = jnp.cos(ang)[:, None, :]
    sin = jnp.sin(ang)[:, None, :]
    xf = x.astype(jnp.float32)
    x1, x2 = xf[..., :half], xf[..., half:]
    return jnp.concatenate([x1 * cos - x2 * sin, x2 * cos + x1 * sin], axis=-1).astype(x.dtype)


def partial_rope(x, pos):
    return jnp.concatenate([rope(x[..., :MOBA_ROT], pos, ROPE_THETA), x[..., MOBA_ROT:]], axis=-1)


def mixer_inputs(h, pos, w_in, g_qnorm, w_uq, g_kvnorm, w_uk):
    B, T, _ = h.shape
    widths = (Q_LORA, KV_LORA, ROPE_DIM, MOBA_W, MOBA_W, MOBA_W, D_MODEL, D_MODEL)
    cuts = np.cumsum(widths)[:-1].tolist()
    q_lat, kv_lat, k_r, q_m, k_m, v_m, gate_a, gate_b = jnp.split(h @ w_in, cuts, axis=-1)
    q = (rmsnorm(q_lat, g_qnorm) @ w_uq).reshape(B, T, MLA_HEADS, MLA_QK)
    q_rope = rope(q[..., NOPE_DIM:], pos, MLA_THETA)
    q_abs = jnp.einsum('bthn,chn->bthc', q[..., :NOPE_DIM], w_uk)
    c_kv = rmsnorm(kv_lat, g_kvnorm)
    k_rope = rope(k_r[:, :, None, :], pos, MLA_THETA)[:, :, 0, :]
    q_m = partial_rope(q_m.reshape(B, T, MOBA_HEADS, MOBA_HD), pos)
    k_m = partial_rope(k_m.reshape(B, T, MOBA_HEADS, MOBA_HD), pos)
    v_m = v_m.reshape(B, T, MOBA_HEADS, MOBA_HD)
    return (q_abs, q_rope, c_kv, k_rope, q_m, k_m, v_m,
            jax.nn.sigmoid(gate_a), jax.nn.sigmoid(gate_b))


def mla_prompt(q_abs, q_rope, c_kv, k_rope):
    B, S, H, C = q_abs.shape
    kpos = jnp.arange(S)

    def block(i):
        q0 = i * Q_BLOCK
        qa = lax.dynamic_slice_in_dim(q_abs, q0, Q_BLOCK, axis=1)
        qr = lax.dynamic_slice_in_dim(q_rope, q0, Q_BLOCK, axis=1)
        s = (jnp.einsum('bqhc,bkc->bhqk', qa, c_kv)
             + jnp.einsum('bqhr,bkr->bhqk', qr, k_rope)).astype(jnp.float32) * MLA_SCALE
        qpos = q0 + jnp.arange(Q_BLOCK)
        s = jnp.where(kpos[None, :] <= qpos[:, None], s, -jnp.inf)
        p = jax.nn.softmax(s, axis=-1).astype(c_kv.dtype)
        return jnp.einsum('bhqk,bkc->bqhc', p, c_kv)

    o = lax.map(block, jnp.arange(S // Q_BLOCK))
    return jnp.moveaxis(o, 0, 1).reshape(B, S, H, C)


def mla_sample(q_abs, q_rope, c_new, kr_new, cache_lat, cache_kr, page_table):
    T = q_abs.shape[1]
    allowed = jnp.arange(PAST_LEN + T)[None, :] <= PAST_LEN + jnp.arange(T)[:, None]

    def one(args):
        qa, qr, cn, kn, pages = args
        c_all = jnp.concatenate([cache_lat[pages].reshape(PAST_LEN, KV_LORA), cn.astype(cache_lat.dtype)], axis=0)
        kr_all = jnp.concatenate([cache_kr[pages].reshape(PAST_LEN, ROPE_DIM), kn.astype(cache_kr.dtype)], axis=0)
        s = (jnp.einsum('thc,kc->htk', qa, c_all)
             + jnp.einsum('thr,kr->htk', qr, kr_all)).astype(jnp.float32) * MLA_SCALE
        s = jnp.where(allowed, s, -jnp.inf)
        p = jax.nn.softmax(s, axis=-1).astype(c_all.dtype)
        return jnp.einsum('htk,kc->thc', p, c_all)

    return lax.map(one, (q_abs, q_rope, c_new, kr_new, page_table))


def moba_prompt(q, k, v):
    B, S, H, D = q.shape
    n_full = S // MOBA_BLOCK
    kb = k[:, :n_full * MOBA_BLOCK].reshape(B, n_full, MOBA_BLOCK, H, D)
    vb = v[:, :n_full * MOBA_BLOCK].reshape(B, n_full, MOBA_BLOCK, H, D)
    means = jnp.mean(kb.astype(jnp.float32), axis=2)
    kb = kb.transpose(0, 3, 1, 2, 4)
    vb = vb.transpose(0, 3, 1, 2, 4)
    bi = jnp.arange(B)[:, None, None, None]
    hi = jnp.arange(H)[None, :, None, None]
    outs = []
    for i in range(S // Q_BLOCK):
        q0, q1 = i * Q_BLOCK, (i + 1) * Q_BLOCK
        own = q0 // MOBA_BLOCK
        k0 = own * MOBA_BLOCK
        qc, ko, vo = q[:, q0:q1], k[:, k0:q1], v[:, k0:q1]
        s_own = jnp.einsum('bqhd,bkhd->bhqk', qc, ko).astype(jnp.float32) * MOBA_SCALE
        causal = (k0 + jnp.arange(q1 - k0))[None, :] <= (q0 + jnp.arange(Q_BLOCK))[:, None]
        s_own = jnp.where(causal, s_own, -jnp.inf)
        if own > 0:
            kk = min(MOBA_TOPK, own)
            s_blk = jnp.einsum('bqhd,bnhd->bhqn', qc.astype(jnp.float32), means[:, :own])
            _, sel = lax.top_k(s_blk, kk)
            ks = kb[bi, hi, sel]
            vs = vb[bi, hi, sel]
            s_sel = jnp.einsum('bqhd,bhqnkd->bhqnk', qc, ks).astype(jnp.float32) * MOBA_SCALE
            s_all = jnp.concatenate([s_sel.reshape(B, H, Q_BLOCK, kk * MOBA_BLOCK), s_own], axis=-1)
            p = jax.nn.softmax(s_all, axis=-1).astype(v.dtype)
            p_sel = p[..., :kk * MOBA_BLOCK].reshape(B, H, Q_BLOCK, kk, MOBA_BLOCK)
            p_own = p[..., kk * MOBA_BLOCK:]
            o = (jnp.einsum('bhqnk,bhqnkd->bqhd', p_sel, vs)
                 + jnp.einsum('bhqk,bkhd->bqhd', p_own, vo))
        else:
            p = jax.nn.softmax(s_own, axis=-1).astype(v.dtype)
            o = jnp.einsum('bhqk,bkhd->bqhd', p, vo)
        outs.append(o)
    return jnp.concatenate(outs, axis=1)


def moba_sample(q, k_new, v_new, cache_k, cache_v, page_table):
    T = q.shape[1]
    H, D = MOBA_HEADS, MOBA_HD
    n_full = PAST_LEN // MOBA_BLOCK
    r = PAST_LEN - n_full * MOBA_BLOCK
    allowed = jnp.arange(r + T)[None, :] <= r + jnp.arange(T)[:, None]
    hi = jnp.arange(H)[:, None, None]

    def one(args):
        qs, kn, vn, pages = args
        kp = cache_k[pages].reshape(PAST_LEN, H, D)
        vp = cache_v[pages].reshape(PAST_LEN, H, D)
        ko = jnp.concatenate([kp[n_full * MOBA_BLOCK:], kn.astype(kp.dtype)], axis=0)
        vo = jnp.concatenate([vp[n_full * MOBA_BLOCK:], vn.astype(vp.dtype)], axis=0)
        s_own = jnp.einsum('thd,khd->htk', qs, ko).astype(jnp.float32) * MOBA_SCALE
        s_own = jnp.where(allowed, s_own, -jnp.inf)
        if n_full > 0:
            kk = min(MOBA_TOPK, n_full)
            kb = kp[:n_full * MOBA_BLOCK].reshape(n_full, MOBA_BLOCK, H, D)
            vb = vp[:n_full * MOBA_BLOCK].reshape(n_full, MOBA_BLOCK, H, D)
            means = jnp.mean(kb.astype(jnp.float32), axis=1)
            s_blk = jnp.einsum('thd,nhd->htn', qs.astype(jnp.float32), means)
            _, sel = lax.top_k(s_blk, kk)
            ks = kb.transpose(2, 0, 1, 3)[hi, sel]
            vs = vb.transpose(2, 0, 1, 3)[hi, sel]
            s_sel = jnp.einsum('thd,htnkd->htnk', qs, ks).astype(jnp.float32) * MOBA_SCALE
            s_all = jnp.concatenate([s_sel.reshape(H, T, kk * MOBA_BLOCK), s_own], axis=-1)
            p = jax.nn.softmax(s_all, axis=-1).astype(vp.dtype)
            p_sel = p[..., :kk * MOBA_BLOCK].reshape(H, T, kk, MOBA_BLOCK)
            p_own = p[..., kk * MOBA_BLOCK:]
            return (jnp.einsum('htnk,htnkd->thd', p_sel, vs)
                    + jnp.einsum('htk,khd->thd', p_own, vo))
        p = jax.nn.softmax(s_own, axis=-1).astype(vp.dtype)
        return jnp.einsum('htk,khd->thd', p, vo)

    return lax.map(one, (q, k_new, v_new, page_table))


def gated_merge(x, o_mla, o_moba, g_a, g_b, w_br_mla, w_br_moba, w_out):
    return x + (g_a * (o_mla @ w_br_mla) + g_b * (o_moba @ w_br_moba)) @ w_out


def conv_ffn(x, hist, g, w_up, conv_w, conv_b, w_down):
    T = x.shape[1]
    u = rmsnorm(x, g) @ w_up
    ue = jnp.concatenate([hist.astype(u.dtype), u], axis=1)
    c = conv_b
    for i in range(CONV_W):
        c = c + conv_w[i] * ue[:, i:i + T]
    a, b = jnp.split(c, 2, axis=-1)
    return x + (jax.nn.silu(a) * b) @ w_down, ue[:, T:]


def setup_inputs(seed: int = 0) -> dict:
    key = jax.random.key(seed)
    ks = jax.random.split(key, 32)
    f32 = jnp.float32
    n_pages = PAST_LEN // PAGE_SIZE
    n_used = DEC_BATCH * n_pages
    n_phys = n_used + n_used // 4

    def nrm(k, shape, s):
        return jax.random.normal(k, shape, f32) * s

    def gain(k, n):
        return 1.0 + 0.01 * jax.random.normal(k, (n,), f32)

    in_width = Q_LORA + KV_LORA + ROPE_DIM + 3 * MOBA_W + 2 * D_MODEL
    page_table = jax.random.permutation(ks[7], n_phys)[:n_used].reshape(DEC_BATCH, n_pages).astype(jnp.int32)
    return {
        'x_prompt': nrm(ks[0], (BATCH, SEQ, D_MODEL), 1.0),
        'x_sample': nrm(ks[1], (DEC_BATCH, DEC_SEQ, D_MODEL), 1.0),
        'cache_mla_latent': nrm(ks[2], (n_phys, PAGE_SIZE, KV_LORA), 1.0),
        'cache_mla_krope': nrm(ks[3], (n_phys, PAGE_SIZE, ROPE_DIM), 1.0),
        'cache_moba_k': nrm(ks[4], (n_phys, PAGE_SIZE, MOBA_HEADS, MOBA_HD), 1.0),
        'cache_moba_v': nrm(ks[5], (n_phys, PAGE_SIZE, MOBA_HEADS, MOBA_HD), 1.0),
        'state_ffn_conv': nrm(ks[6], (DEC_BATCH, CONV_W - 1, 2 * D_FF), 1.0),
        'page_table': page_table,
        'g_attn_norm': gain(ks[8], D_MODEL),
        'w_in': nrm(ks[9], (D_MODEL, in_width), D_MODEL ** -0.5),
        'g_qnorm': gain(ks[10], Q_LORA),
        'w_uq': nrm(ks[11], (Q_LORA, MLA_HEADS * MLA_QK), Q_LORA ** -0.5),
        'g_kvnorm': gain(ks[12], KV_LORA),
        'w_uk': nrm(ks[13], (KV_LORA, MLA_HEADS, NOPE_DIM), KV_LORA ** -0.5),
        'w_uv': nrm(ks[14], (KV_LORA, MLA_HEADS, V_DIM), KV_LORA ** -0.5),
        'w_br_mla': nrm(ks[15], (MLA_W, D_MODEL), MLA_W ** -0.5),
        'w_br_moba': nrm(ks[16], (MOBA_W, D_MODEL), MOBA_W ** -0.5),
        'w_out': nrm(ks[17], (D_MODEL, D_MODEL), D_MODEL ** -0.5),
        'g_ffn_norm': gain(ks[18], D_MODEL),
        'w_up': nrm(ks[19], (D_MODEL, 2 * D_FF), D_MODEL ** -0.5),
        'conv_w': nrm(ks[20], (CONV_W, 2 * D_FF), CONV_W ** -0.5),
        'conv_b': nrm(ks[21], (2 * D_FF,), 0.01),
        'w_down': nrm(ks[22], (D_FF, D_MODEL), D_FF ** -0.5),
        'g_final': gain(ks[23], D_MODEL),
    }


def reference(x_prompt, x_sample, cache_mla_latent, cache_mla_krope, cache_moba_k, cache_moba_v,
              state_ffn_conv, page_table, g_attn_norm, w_in, g_qnorm, w_uq, g_kvnorm, w_uk, w_uv,
              w_br_mla, w_br_moba, w_out, g_ffn_norm, w_up, conv_w, conv_b, w_down, g_final):
    B, S = x_prompt.shape[0], x_prompt.shape[1]
    DB, T = x_sample.shape[0], x_sample.shape[1]
    pos_p = jnp.arange(S, dtype=jnp.int32)
    pos_s = PAST_LEN + jnp.arange(T, dtype=jnp.int32)
    xp, xs = x_prompt, x_sample
    for _ in range(DEPTH):
        qa, qr, c_p, kr_p, qm, k_p, v_p, ga, gb = mixer_inputs(
            rmsnorm(xp, g_attn_norm), pos_p, w_in, g_qnorm, w_uq, g_kvnorm, w_uk)
        o_mla = jnp.einsum('bshc,chv->bshv', mla_prompt(qa, qr, c_p, kr_p), w_uv).reshape(B, S, MLA_W)
        o_moba = moba_prompt(qm, k_p, v_p).reshape(B, S, MOBA_W)
        xp = gated_merge(xp, o_mla, o_moba, ga, gb, w_br_mla, w_br_moba, w_out)
        xp, conv_p = conv_ffn(xp, jnp.zeros((B, CONV_W - 1, 2 * D_FF), xp.dtype),
                              g_ffn_norm, w_up, conv_w, conv_b, w_down)
        qa_s, qr_s, c_s, kr_s, qm_s, k_s, v_s, ga_s, gb_s = mixer_inputs(
            rmsnorm(xs, g_attn_norm), pos_s, w_in, g_qnorm, w_uq, g_kvnorm, w_uk)
        o_lat_s = mla_sample(qa_s, qr_s, c_s, kr_s, cache_mla_latent, cache_mla_krope, page_table)
        o_mla_s = jnp.einsum('bshc,chv->bshv', o_lat_s, w_uv).reshape(DB, T, MLA_W)
        o_moba_s = moba_sample(qm_s, k_s, v_s, cache_moba_k, cache_moba_v, page_table).reshape(DB, T, MOBA_W)
        xs = gated_merge(xs, o_mla_s, o_moba_s, ga_s, gb_s, w_br_mla, w_br_moba, w_out)
        xs, conv_s = conv_ffn(xs, state_ffn_conv, g_ffn_norm, w_up, conv_w, conv_b, w_down)
    y_prompt = rmsnorm(xp, g_final)
    y_sample = rmsnorm(xs, g_final)
    return (y_prompt, y_sample, c_p, kr_p, k_p, v_p, conv_p, c_s, kr_s, k_s, v_s, conv_s)
```

```python
import functools

import jax
import jax.numpy as jnp
import numpy as np
from jax import lax
from jax.experimental import pallas as pl
from jax.experimental.pallas import tpu as pltpu

D_MODEL = 1024
PAST_LEN = 8192
PAGE_SIZE = 128

MLA_HEADS = 8
Q_LORA = 512
KV_LORA = 256
NOPE_DIM = 64
ROPE_DIM = 32
V_DIM = 64
MLA_QK = NOPE_DIM + ROPE_DIM
MLA_W = MLA_HEADS * V_DIM
MLA_SCALE = MLA_QK ** -0.5
MLA_THETA = 10000.0

MOBA_HEADS = 8
MOBA_HD = 64
MOBA_W = MOBA_HEADS * MOBA_HD
MOBA_ROT = MOBA_HD // 4
MOBA_BLOCK = 256
MOBA_TOPK = 3
MOBA_SCALE = MOBA_HD ** -0.5
ROPE_THETA = 500000.0

D_FF = 2816
CONV_W = 3
EPS = 1e-6

LANES = 128
ROW_TILE = 256
VMEM_LIMIT = 56 * 1024 * 1024
NEG_INF = float("-inf")

BF16 = jnp.bfloat16
F32 = jnp.float32


def _cparams(*sem):
    return pltpu.CompilerParams(dimension_semantics=sem, vmem_limit_bytes=VMEM_LIMIT)


def _dot(a, b):
    return jnp.dot(a, b, preferred_element_type=F32)


def _dot_nt(a, b):
    return lax.dot_general(a, b, (((1,), (1,)), ((), ())), preferred_element_type=F32)


def _rms(x, g):
    return x * lax.rsqrt(jnp.mean(x * x, axis=-1, keepdims=True) + EPS) * g


def _angles(pos, d, theta):
    half = d // 2
    inv = 1.0 / (theta ** (jnp.arange(half, dtype=F32) * (2.0 / d)))
    ang = pos.astype(F32)[:, None] * inv[None, :]
    return jnp.cos(ang), jnp.sin(ang)


def _row_table(cos, sin, group, lanes):
    half = cos.shape[1]
    p = cos.shape[0]
    lane = np.arange(lanes) % group
    first = lane < half
    second = (lane >= half) & (lane < 2 * half)
    idx = np.where(first, lane, np.where(second, lane - half, 0))
    c = jnp.where((first | second)[None, :], cos[:, idx], 1.0)
    sa = jnp.where(first[None, :], -sin[:, idx], 0.0)
    sb = jnp.where(second[None, :], sin[:, idx], 0.0)
    del p
    return jnp.concatenate([c, sa, sb], axis=1).astype(F32)


def _rope_tables(pos):
    cm, sm = _angles(pos, MOBA_ROT, ROPE_THETA)
    cr, sr = _angles(pos, ROPE_DIM, MLA_THETA)
    t_qm = _row_table(cm, sm, MOBA_HD, LANES)
    t_qr = _row_table(cr, sr, ROPE_DIM, LANES)
    t_kt = jnp.concatenate([cm.T, sm.T], axis=0)
    t_rt = jnp.concatenate([cr.T, sr.T], axis=0)
    return t_qm, t_qr, t_kt, t_rt


def _rope_rows(x, tab, half):
    c, sa, sb = tab[:, :LANES], tab[:, LANES:2 * LANES], tab[:, 2 * LANES:]
    outs = []
    for k in range(x.shape[1] // LANES):
        xk = x[:, k * LANES:(k + 1) * LANES]
        up = pltpu.roll(xk, LANES - half, axis=1)
        dn = pltpu.roll(xk, half, axis=1)
        outs.append(xk * c + up * sa + dn * sb)
    return jnp.concatenate(outs, axis=1) if len(outs) > 1 else outs[0]


W_ROW_COLS = Q_LORA + KV_LORA + MOBA_W + 2 * D_MODEL
W_T_ROWS = 2 * MOBA_W + ROPE_DIM


def _proj_kernel(x_ref, g_ref, wrow_ref, wt_ref, gq_ref, wuq_ref, gkv_ref, wuk_ref,
                 tqm_ref, tqr_ref, tkt_ref, trt_ref,
                 ckv_ref, ckvb_ref, krt_ref, kr8_ref, kt_ref, vt_ref,
                 qm_ref, qa_ref, qr_ref, ga_ref, gb_ref):
    x = x_ref[...]
    h = _rms(x, g_ref[...]).astype(BF16)

    o = 0
    q_lat = _dot(h, wrow_ref[:, o:o + Q_LORA]); o += Q_LORA
    kv_lat = _dot(h, wrow_ref[:, o:o + KV_LORA]); o += KV_LORA
    q_m = _dot(h, wrow_ref[:, o:o + MOBA_W]); o += MOBA_W
    ga_ref[...] = jax.nn.sigmoid(_dot(h, wrow_ref[:, o:o + D_MODEL])); o += D_MODEL
    gb_ref[...] = jax.nn.sigmoid(_dot(h, wrow_ref[:, o:o + D_MODEL]))

    ckv = _rms(kv_lat, gkv_ref[...])
    ckv_ref[...] = ckv
    ckvb_ref[...] = ckv.astype(BF16)

    qm_ref[...] = (_rope_rows(q_m, tqm_ref[...], MOBA_ROT // 2) * MOBA_SCALE).astype(BF16)

    qn = _rms(q_lat, gq_ref[...]).astype(BF16)
    q = _dot(qn, wuq_ref[...])
    q_nope = q[:, :MLA_HEADS * NOPE_DIM].astype(BF16)
    qr_ref[...] = _rope_rows(q[:, MLA_HEADS * NOPE_DIM:], tqr_ref[...], ROPE_DIM // 2).astype(BF16)
    for j in range(MLA_HEADS // 2):
        qa2 = _dot(q_nope[:, j * LANES:(j + 1) * LANES], wuk_ref[j])
        qa_ref[2 * j] = qa2[:, :KV_LORA].astype(BF16)
        qa_ref[2 * j + 1] = qa2[:, KV_LORA:].astype(BF16)

    yt = _dot_nt(wt_ref[...], h)
    tkt = tkt_ref[...]
    cm, sm = tkt[:MOBA_ROT // 2], tkt[MOBA_ROT // 2:]
    hr = MOBA_ROT // 2
    for hd in range(MOBA_HEADS):
        r0 = hd * MOBA_HD
        x1 = yt[r0:r0 + hr]
        x2 = yt[r0 + hr:r0 + 2 * hr]
        kt_ref[r0:r0 + hr, :] = x1 * cm - x2 * sm
        kt_ref[r0 + hr:r0 + 2 * hr, :] = x2 * cm + x1 * sm
        kt_ref[r0 + 2 * hr:r0 + MOBA_HD, :] = yt[r0 + 2 * hr:r0 + MOBA_HD]
    vt_ref[...] = yt[MOBA_W:2 * MOBA_W]
    trt = trt_ref[...]
    cr, sr = trt[:ROPE_DIM // 2], trt[ROPE_DIM // 2:]
    y1 = yt[2 * MOBA_W:2 * MOBA_W + ROPE_DIM // 2]
    y2 = yt[2 * MOBA_W + ROPE_DIM // 2:]
    krt = jnp.concatenate([y1 * cr - y2 * sr, y2 * cr + y1 * sr], axis=0)
    krt_ref[...] = krt
    krb = krt.astype(BF16)
    for hd in range(MLA_HEADS):
        kr8_ref[hd * ROPE_DIM:(hd + 1) * ROPE_DIM, :] = krb


def _proj(x2d, tables, tiles_per_seq, wts):
    n = x2d.shape[0]
    tm = ROW_TILE
    nseq = n // (tm * tiles_per_seq)
    seq_len = tm * tiles_per_seq
    t_qm, t_qr, t_kt, t_rt = tables
    g_attn, w_row, w_t, g_q, w_uq, g_kv, w_ukp = wts

    def full(a):
        nd = a.ndim
        return pl.BlockSpec(a.shape, lambda i: (0,) * nd)

    row = lambda w: pl.BlockSpec((tm, w), lambda i: (i, 0))
    tab_row = lambda w: pl.BlockSpec((tm, w), lambda i: (i % tiles_per_seq, 0))
    tab_col = lambda r: pl.BlockSpec((r, tm), lambda i: (0, i % tiles_per_seq))
    col = lambda r: pl.BlockSpec((None, r, tm), lambda i: (i // tiles_per_seq, 0, i % tiles_per_seq))

    out_shape = (
        jax.ShapeDtypeStruct((n, KV_LORA), F32),
        jax.ShapeDtypeStruct((n, KV_LORA), BF16),
        jax.ShapeDtypeStruct((nseq, ROPE_DIM, seq_len), F32),
        jax.ShapeDtypeStruct((nseq, tiles_per_seq, MLA_HEADS * ROPE_DIM, tm), BF16),
        jax.ShapeDtypeStruct((nseq, MOBA_W, seq_len), F32),
        jax.ShapeDtypeStruct((nseq, MOBA_W, seq_len), F32),
        jax.ShapeDtypeStruct((n, MOBA_W), BF16),
        jax.ShapeDtypeStruct((nseq, MLA_HEADS, seq_len, KV_LORA), BF16),
        jax.ShapeDtypeStruct((n, MLA_HEADS * ROPE_DIM), BF16),
        jax.ShapeDtypeStruct((n, D_MODEL), F32),
        jax.ShapeDtypeStruct((n, D_MODEL), F32),
    )
    out_specs = (
        row(KV_LORA), row(KV_LORA), col(ROPE_DIM),
        pl.BlockSpec((None, None, MLA_HEADS * ROPE_DIM, tm),
                     lambda i: (i // tiles_per_seq, i % tiles_per_seq, 0, 0)),
        col(MOBA_W), col(MOBA_W),
        row(MOBA_W),
        pl.BlockSpec((None, MLA_HEADS, tm, KV_LORA), lambda i: (i // tiles_per_seq, 0, i % tiles_per_seq, 0)),
        row(MLA_HEADS * ROPE_DIM), row(D_MODEL), row(D_MODEL),
    )
    in_specs = [row(D_MODEL), full(g_attn), full(w_row), full(w_t), full(g_q), full(w_uq), full(g_kv),
                full(w_ukp), tab_row(3 * LANES), tab_row(3 * LANES), tab_col(MOBA_ROT), tab_col(ROPE_DIM)]
    return pl.pallas_call(
        _proj_kernel, grid=(n // tm,), in_specs=in_specs, out_specs=out_specs, out_shape=out_shape,
        compiler_params=_cparams("arbitrary"), name="proj",
    )(x2d, g_attn, w_row, w_t, g_q, w_uq, g_kv, w_ukp, t_qm, t_qr, t_kt, t_rt)


def _prep_proj_weights(g_attn_norm, w_in, g_qnorm, w_uq, g_kvnorm, w_uk):
    o_q, o_kv, o_kr = 0, Q_LORA, Q_LORA + KV_LORA
    o_qm = o_kr + ROPE_DIM
    o_km, o_vm = o_qm + MOBA_W, o_qm + 2 * MOBA_W
    o_ga = o_vm + MOBA_W
    w_row = jnp.concatenate([w_in[:, o_q:o_kr], w_in[:, o_qm:o_km], w_in[:, o_ga:]], axis=1).astype(BF16)
    w_t = jnp.concatenate([w_in[:, o_km:o_vm], w_in[:, o_vm:o_ga], w_in[:, o_kr:o_qm]], axis=1).T.astype(BF16)
    wq = w_uq.reshape(Q_LORA, MLA_HEADS, MLA_QK)
    w_uq_p = jnp.concatenate([wq[:, :, :NOPE_DIM].reshape(Q_LORA, -1),
                              wq[:, :, NOPE_DIM:].reshape(Q_LORA, -1)], axis=1).astype(BF16)
    wk = jnp.transpose(w_uk, (1, 2, 0)).astype(BF16)
    z = jnp.zeros((NOPE_DIM, KV_LORA), BF16)
    w_ukp = jnp.stack([
        jnp.concatenate([jnp.concatenate([wk[2 * j], z], axis=1),
                         jnp.concatenate([z, wk[2 * j + 1]], axis=1)], axis=0)
        for j in range(MLA_HEADS // 2)])
    return (g_attn_norm.reshape(1, -1), w_row, w_t, g_qnorm.reshape(1, -1), w_uq_p,
            g_kvnorm.reshape(1, -1), w_ukp)


ATT_TILE = 256


def _mla_prompt_kernel(qa_ref, qr_ref, ckv_ref, kr8_ref, wuv_ref, o_ref, qrs_ref, m_ref, l_ref, acc_ref):
    i = pl.program_id(1)
    t = ATT_TILE
    rows = MLA_HEADS * t
    qa = qa_ref[...].reshape(rows, KV_LORA)
    qr = qr_ref[...]
    lane_head = lax.broadcasted_iota(jnp.int32, qr.shape, 1) // ROPE_DIM
    for h in range(MLA_HEADS):
        qrs_ref[h * t:(h + 1) * t, :] = jnp.where(lane_head == h, qr, jnp.zeros_like(qr))
    m_ref[...] = jnp.full(m_ref.shape, NEG_INF, F32)
    l_ref[...] = jnp.zeros(l_ref.shape, F32)
    acc_ref[...] = jnp.zeros(acc_ref.shape, F32)

    def step(j, mask):
        c = ckv_ref[pl.ds(pl.multiple_of(j * t, t), t), :]
        s = (_dot_nt(qa, c) + _dot(qrs_ref[...], kr8_ref[j])) * MLA_SCALE
        if mask is not None:
            s = jnp.where(mask, s, NEG_INF)
        m_old = m_ref[...]
        m_new = jnp.maximum(m_old, jnp.max(s, axis=1, keepdims=True))
        alpha = jnp.exp(m_old - m_new)
        p = jnp.exp(s - m_new)
        l_ref[...] = alpha * l_ref[...] + jnp.sum(p, axis=1, keepdims=True)
        acc_ref[...] = alpha * acc_ref[...] + _dot(p.astype(BF16), c)
        m_ref[...] = m_new

    def body(j, carry):
        step(j, None)
        return carry

    lax.fori_loop(0, i, body, 0)
    qpos = lax.broadcasted_iota(jnp.int32, (rows, t), 0) % t
    kpos = lax.broadcasted_iota(jnp.int32, (rows, t), 1)
    step(i, kpos <= qpos)

    o_lat = (acc_ref[...] / l_ref[...]).astype(BF16)
    for j in range(MLA_HEADS // 2):
        o2 = (_dot(o_lat[2 * j * t:(2 * j + 1) * t], wuv_ref[2 * j])
              + _dot(o_lat[(2 * j + 1) * t:(2 * j + 2) * t], wuv_ref[2 * j + 1]))
        o_ref[:, j * LANES:(j + 1) * LANES] = o2.astype(BF16)


def _prep_wuv(w_uv):
    w = jnp.transpose(w_uv, (1, 0, 2)).astype(BF16)
    z = jnp.zeros_like(w)
    even = jnp.concatenate([w, z], axis=2)
    odd = jnp.concatenate([z, w], axis=2)
    is_even = (jnp.arange(MLA_HEADS) % 2 == 0)[:, None, None]
    return jnp.where(is_even, even, odd)


def _mla_prompt(qa, qr, ckvb, kr8, wuv, b, s):
    t = ATT_TILE
    nq = s // t
    rows = MLA_HEADS * t
    return pl.pallas_call(
        _mla_prompt_kernel, grid=(b, nq),
        in_specs=[
            pl.BlockSpec((None, MLA_HEADS, t, KV_LORA), lambda bi, i: (bi, 0, i, 0)),
            pl.BlockSpec((t, MLA_HEADS * ROPE_DIM), lambda bi, i: (bi * nq + i, 0)),
            pl.BlockSpec((s, KV_LORA), lambda bi, i: (bi, 0)),
            pl.BlockSpec((None, nq, MLA_HEADS * ROPE_DIM, t), lambda bi, i: (bi, 0, 0, 0)),
            pl.BlockSpec(wuv.shape, lambda bi, i: (0, 0, 0)),
        ],
        out_specs=pl.BlockSpec((t, MLA_W), lambda bi, i: (bi * nq + i, 0)),
        out_shape=jax.ShapeDtypeStruct((b * s, MLA_W), BF16),
        scratch_shapes=[pltpu.VMEM((rows, MLA_HEADS * ROPE_DIM), BF16), pltpu.VMEM((rows, 1), F32),
                        pltpu.VMEM((rows, 1), F32), pltpu.VMEM((rows, KV_LORA), F32)],
        compiler_params=_cparams("arbitrary", "arbitrary"), name="mla_p",
    )(qa, qr, ckvb, kr8, wuv)


def _moba_prompt_kernel(nblk, q_ref, kt_ref, vt_ref, o_ref, s_ref, bs_ref, m_ref, l_ref, acc_ref):
    i = pl.program_id(2)
    t = MOBA_BLOCK
    q = q_ref[...]
    lane = lax.broadcasted_iota(jnp.int32, (t, LANES), 1)
    causal = lax.broadcasted_iota(jnp.int32, (t, t), 1) <= lax.broadcasted_iota(jnp.int32, (t, t), 0)
    outs = []
    for half in range(2):
        qh = jnp.where((lane >= MOBA_HD) == bool(half), q, jnp.zeros_like(q))
        bs_ref[...] = jnp.full(bs_ref.shape, NEG_INF, F32)
        m_ref[...] = jnp.full(m_ref.shape, NEG_INF, F32)
        l_ref[...] = jnp.zeros(l_ref.shape, F32)
        acc_ref[...] = jnp.zeros(acc_ref.shape, F32)

        for n in range(nblk):
            @pl.when(n <= i)
            def _(n=n):
                s = _dot(qh, kt_ref[:, n * t:(n + 1) * t].astype(BF16))
                s_ref[:, n * t:(n + 1) * t] = jnp.where(jnp.logical_or(n < i, causal), s, NEG_INF)

                @pl.when(n < i)
                def _():
                    bs_ref[:, n:n + 1] = jnp.sum(s, axis=1, keepdims=True)

        bs = bs_ref[...]
        for n in range(nblk):
            @pl.when(n <= i)
            def _(n=n):
                col = bs[:, n:n + 1]
                beats = jnp.logical_or(bs > col, jnp.logical_and(bs == col, lane < n))
                rank = jnp.sum(beats.astype(F32), axis=1, keepdims=True)
                keep = jnp.logical_or(rank < MOBA_TOPK, n >= i)
                s = jnp.where(keep, s_ref[:, n * t:(n + 1) * t], NEG_INF)
                s_ref[:, n * t:(n + 1) * t] = s
                m_ref[...] = jnp.maximum(m_ref[...], jnp.max(s, axis=1, keepdims=True))

        for n in range(nblk):
            @pl.when(n <= i)
            def _(n=n):
                p = jnp.exp(s_ref[:, n * t:(n + 1) * t] - m_ref[...])
                l_ref[...] += jnp.sum(p, axis=1, keepdims=True)
                acc_ref[...] += _dot_nt(p.astype(BF16), vt_ref[:, n * t:(n + 1) * t].astype(BF16))

        outs.append(acc_ref[...] / l_ref[...])
    o_ref[...] = jnp.where(lane < MOBA_HD, outs[0], outs[1]).astype(BF16)


def _moba_prompt(qm, kt, vt, b, s):
    t = MOBA_BLOCK
    nq = s // t
    npair = MOBA_W // LANES
    return pl.pallas_call(
        functools.partial(_moba_prompt_kernel, nq), grid=(b, npair, nq),
        in_specs=[
            pl.BlockSpec((t, LANES), lambda bi, j, i: (bi * nq + i, j)),
            pl.BlockSpec((None, LANES, s), lambda bi, j, i: (bi, j, 0)),
            pl.BlockSpec((None, LANES, s), lambda bi, j, i: (bi, j, 0)),
        ],
        out_specs=pl.BlockSpec((t, LANES), lambda bi, j, i: (bi * nq + i, j)),
        out_shape=jax.ShapeDtypeStruct((b * s, MOBA_W), BF16),
        scratch_shapes=[pltpu.VMEM((t, s), F32), pltpu.VMEM((t, LANES), F32), pltpu.VMEM((t, 1), F32),
                        pltpu.VMEM((t, 1), F32), pltpu.VMEM((t, LANES), F32)],
        compiler_params=_cparams("arbitrary", "arbitrary", "arbitrary"), name="moba_p",
    )(qm, kt, vt)


FF_CHUNK = D_FF // 2
HIST_ROWS = 8
SAMPLE_TILE = 128


def _tail_kernel(seq_tiles, seq_rows, *refs):
    per_row_hist = seq_tiles == 0
    if per_row_hist:
        (x_ref, olat_ref, ob_ref, ga_ref, gb_ref, hist_ref, wuv_ref, wa_ref, wb_ref, wo_ref, gf_ref,
         wup_ref, cw_ref, cb_ref, wdn_ref, gfin_ref, y_ref, u_ref, ue_ref) = refs
        om = jnp.concatenate(
            [_dot(olat_ref[2 * j], wuv_ref[2 * j]) + _dot(olat_ref[2 * j + 1], wuv_ref[2 * j + 1])
             for j in range(MLA_HEADS // 2)], axis=1).astype(BF16)
    else:
        (x_ref, om_ref, ob_ref, ga_ref, gb_ref, wa_ref, wb_ref, wo_ref, gf_ref, wup_ref,
         cw_ref, cb_ref, wdn_ref, gfin_ref, y_ref, conv_ref, ue_ref, carry_ref) = refs
        om = om_ref[...]
    i = pl.program_id(0)
    tm = x_ref.shape[0]
    cw = FF_CHUNK

    a = _dot(om, wa_ref[...])
    b = _dot(ob_ref[...], wb_ref[...])
    mg = (ga_ref[...] * a + gb_ref[...] * b).astype(BF16)
    x1 = x_ref[...] + _dot(mg, wo_ref[...])
    hn = _rms(x1, gf_ref[...]).astype(BF16)

    if per_row_hist:
        tpos = lax.broadcasted_iota(jnp.int32, (tm, cw), 0) % seq_rows
        ue_ref[0:HIST_ROWS, :] = jnp.zeros((HIST_ROWS, cw), F32)
    else:
        first = (i % seq_tiles) == 0

        @pl.when(i == 0)
        def _():
            carry_ref[...] = jnp.zeros(carry_ref.shape, F32)

    acc = x1
    for c in range(D_FF // cw):
        halves = []
        for half in range(2):
            c0 = half * D_FF + c * cw
            u = _dot(hn, wup_ref[:, c0:c0 + cw])
            if not per_row_hist:
                ue_ref[0:HIST_ROWS, :] = jnp.where(first, 0.0, carry_ref[:, c0:c0 + cw])
            ue_ref[HIST_ROWS:, :] = u
            u1 = ue_ref[HIST_ROWS - 1:HIST_ROWS - 1 + tm, :]
            u2 = ue_ref[HIST_ROWS - 2:HIST_ROWS - 2 + tm, :]
            if per_row_hist:
                hh = hist_ref[:, c0:c0 + cw]
                u1 = jnp.where(tpos < 1, pltpu.roll(hh, tm - 1, axis=0), u1)
                u2 = jnp.where(tpos < 2, hh, u2)
                u_ref[:, c0:c0 + cw] = u
            else:
                carry_ref[:, c0:c0 + cw] = u[tm - HIST_ROWS:, :]
                conv_ref[:, c0:c0 + cw] = u[tm - (CONV_W - 1):, :]
            halves.append(cb_ref[:, c0:c0 + cw] + cw_ref[0:1, c0:c0 + cw] * u2
                          + cw_ref[1:2, c0:c0 + cw] * u1 + cw_ref[2:3, c0:c0 + cw] * u)
        act = (jax.nn.silu(halves[0]) * halves[1]).astype(BF16)
        acc = acc + _dot(act, wdn_ref[c * cw:(c + 1) * cw, :])
    y_ref[...] = _rms(acc, gfin_ref[...])


def _tail(x2d, o_mla, o_moba, ga, gb, wts, seq_tiles, hist=None, seq_rows=0, wuv=None, tm=ROW_TILE):
    n = x2d.shape[0]
    row = lambda w: pl.BlockSpec((tm, w), lambda i: (i, 0))

    def const(a):
        nd = a.ndim
        return pl.BlockSpec(a.shape, lambda i: (0,) * nd, pipeline_mode=pl.Buffered(1))

    acts = [x2d, o_mla, o_moba, ga, gb]
    act_specs = [row(D_MODEL), row(MLA_W), row(MOBA_W), row(D_MODEL), row(D_MODEL)]
    scratch = [pltpu.VMEM((tm + HIST_ROWS, FF_CHUNK), F32)]
    if seq_tiles == 0:
        acts.append(hist)
        act_specs[1] = pl.BlockSpec((MLA_HEADS, tm, KV_LORA), lambda i: (0, i, 0))
        act_specs.append(row(2 * D_FF))
        wts = (wuv,) + tuple(wts)
        out_shape = (jax.ShapeDtypeStruct((n, D_MODEL), F32), jax.ShapeDtypeStruct((n, 2 * D_FF), F32))
        out_specs = (row(D_MODEL), row(2 * D_FF))
    else:
        nseq = n // (tm * seq_tiles)
        out_shape = (jax.ShapeDtypeStruct((n, D_MODEL), F32),
                     jax.ShapeDtypeStruct((nseq, CONV_W - 1, 2 * D_FF), F32))
        out_specs = (row(D_MODEL), pl.BlockSpec((None, CONV_W - 1, 2 * D_FF), lambda i: (i // seq_tiles, 0, 0)))
        scratch.append(pltpu.VMEM((HIST_ROWS, 2 * D_FF), F32))
    return pl.pallas_call(
        functools.partial(_tail_kernel, seq_tiles, seq_rows), grid=(n // tm,),
        in_specs=act_specs + [const(w) for w in wts], out_specs=out_specs, out_shape=out_shape,
        scratch_shapes=scratch, compiler_params=_cparams("arbitrary"), name="tail",
    )(*acts, *wts)


def _prep_tail_weights(w_br_mla, w_br_moba, w_out, g_ffn_norm, w_up, conv_w, conv_b, w_down, g_final):
    return (w_br_mla.astype(BF16), w_br_moba.astype(BF16), w_out.astype(BF16), g_ffn_norm.reshape(1, -1),
            w_up.astype(BF16), conv_w, conv_b.reshape(1, -1), w_down.astype(BF16), g_final.reshape(1, -1))


PAGE_CHUNK = 8


def _page_stream(pt_ref, hbm_bufs, sem, n_chunks, compute):
    d = pl.program_id(0)
    total = pl.num_programs(0) * n_chunks

    def copies(g, slot):
        dd = g // n_chunks
        c = g % n_chunks
        out = []
        for k in range(PAGE_CHUNK):
            page = pt_ref[dd, c * PAGE_CHUNK + k]
            for idx, (hbm, buf) in enumerate(hbm_bufs):
                out.append(pltpu.make_async_copy(hbm.at[page], buf.at[slot, k], sem.at[slot, idx]))
        return out

    @pl.when(d == 0)
    def _():
        for cp in copies(0, 0):
            cp.start()

    def body(c, carry):
        g = d * n_chunks + c
        slot = g % 2

        @pl.when(g + 1 < total)
        def _():
            for cp in copies(g + 1, 1 - slot):
                cp.start()

        for cp in copies(g, slot):
            cp.wait()
        compute(slot, c)
        return carry

    lax.fori_loop(0, n_chunks, body, 0)


def _mla_sample_kernel(n_chunks, pt_ref, qa_ref, qr_ref, cn_ref, krn_ref, lat_hbm, kr_hbm, o_ref,
                       lat_buf, kr_buf, sem, m_ref, l_ref, acc_ref):
    qa = qa_ref[...]
    qr = qr_ref[...]
    rows = qa.shape[0]
    n_new = cn_ref.shape[0]
    m_ref[...] = jnp.full(m_ref.shape, NEG_INF, F32)
    l_ref[...] = jnp.zeros(l_ref.shape, F32)
    acc_ref[...] = jnp.zeros(acc_ref.shape, F32)

    def compute(slot, c):
        del c
        cb = lat_buf[slot].reshape(PAGE_CHUNK * PAGE_SIZE, KV_LORA).astype(BF16)
        s_rope = jnp.concatenate([_dot(qr, kr_buf[slot, k].astype(BF16)) for k in range(PAGE_CHUNK)], axis=1)
        s = (_dot_nt(qa, cb) + s_rope) * MLA_SCALE
        m_old = m_ref[...]
        m_new = jnp.maximum(m_old, jnp.max(s, axis=1, keepdims=True))
        alpha = jnp.exp(m_old - m_new)
        p = jnp.exp(s - m_new)
        l_ref[...] = alpha * l_ref[...] + jnp.sum(p, axis=1, keepdims=True)
        acc_ref[...] = alpha * acc_ref[...] + _dot(p.astype(BF16), cb)
        m_ref[...] = m_new

    _page_stream(pt_ref, [(lat_hbm, lat_buf), (kr_hbm, kr_buf)], sem, n_chunks, compute)

    qaf = qa.astype(F32)
    qrf = qr.astype(F32)
    cn = cn_ref[...]
    krn = krn_ref[...]
    tq = lax.broadcasted_iota(jnp.int32, (rows, 1), 0) % n_new
    s_new = []
    for t in range(n_new):
        st = (jnp.sum(qaf * cn[t:t + 1], axis=1, keepdims=True)
              + jnp.sum(qrf * krn[t:t + 1], axis=1, keepdims=True)) * MLA_SCALE
        s_new.append(jnp.where(t <= tq, st, NEG_INF))
    m_old = m_ref[...]
    m_new = functools.reduce(jnp.maximum, s_new, m_old)
    alpha = jnp.exp(m_old - m_new)
    l = alpha * l_ref[...]
    acc = alpha * acc_ref[...]
    for t in range(n_new):
        p = jnp.exp(s_new[t] - m_new)
        l = l + p
        acc = acc + p * cn[t:t + 1]
    o_ref[...] = (acc / l).astype(o_ref.dtype)


def _mla_sample(page_table, qa_rows, qr_rows, c_new, kr_new, cache_lat, cache_kr_t):
    db, n_pages = page_table.shape
    n_chunks = n_pages // PAGE_CHUNK
    rows = qa_rows.shape[1]
    t_new = c_new.shape[1]
    per_d = lambda *shape: pl.BlockSpec((None,) + shape, lambda d, pt: (d,) + (0,) * len(shape))
    grid_spec = pltpu.PrefetchScalarGridSpec(
        num_scalar_prefetch=1, grid=(db,),
        in_specs=[per_d(rows, KV_LORA), per_d(rows, ROPE_DIM), per_d(t_new, KV_LORA), per_d(t_new, ROPE_DIM),
                  pl.BlockSpec(memory_space=pl.ANY), pl.BlockSpec(memory_space=pl.ANY)],
        out_specs=per_d(rows, KV_LORA),
        scratch_shapes=[pltpu.VMEM((2, PAGE_CHUNK, PAGE_SIZE, KV_LORA), F32),
                        pltpu.VMEM((2, PAGE_CHUNK, ROPE_DIM, PAGE_SIZE), F32),
                        pltpu.SemaphoreType.DMA((2, 2)),
                        pltpu.VMEM((rows, 1), F32), pltpu.VMEM((rows, 1), F32), pltpu.VMEM((rows, KV_LORA), F32)])
    return pl.pallas_call(
        functools.partial(_mla_sample_kernel, n_chunks), grid_spec=grid_spec,
        out_shape=jax.ShapeDtypeStruct((db, rows, KV_LORA), BF16),
        compiler_params=_cparams("arbitrary"), name="mla_s",
    )(page_table, qa_rows, qr_rows, c_new, kr_new, cache_lat, cache_kr_t)


PAGES_PER_BLOCK = MOBA_BLOCK // PAGE_SIZE


def _moba_sample_scores_kernel(n_chunks, pt_ref, q_ref, kn_ref, k_hbm, p_ref, pown_ref, k_buf, sem, s_ref):
    q = q_ref[...]
    rows = q.shape[0]
    n_new = kn_ref.shape[0]
    n_pages = n_chunks * PAGE_CHUNK
    n_blocks = n_pages // PAGES_PER_BLOCK

    def compute(slot, c):
        for k in range(PAGE_CHUNK):
            s_ref[c * PAGE_CHUNK + k] = _dot(q, k_buf[slot, k].astype(BF16))

    _page_stream(pt_ref, [(k_hbm, k_buf)], sem, n_chunks, compute)

    lane = lax.broadcasted_iota(jnp.int32, (rows, LANES), 1)
    bsum = [sum(jnp.sum(s_ref[PAGES_PER_BLOCK * n + k], axis=1, keepdims=True) for k in range(PAGES_PER_BLOCK))
            for n in range(n_blocks)]
    bs = jnp.full((rows, LANES), NEG_INF, F32)
    for n in range(n_blocks):
        bs = jnp.where(lane == n, bsum[n], bs)
    keep = []
    for n in range(n_blocks):
        beats = jnp.logical_or(bs > bsum[n], jnp.logical_and(bs == bsum[n], lane < n))
        keep.append(jnp.sum(beats.astype(F32), axis=1, keepdims=True) < MOBA_TOPK)

    qf = q.astype(F32)
    kn = kn_ref[...]
    tq = lax.broadcasted_iota(jnp.int32, (rows, 1), 0) // MOBA_HEADS
    s_new = [jnp.where(t <= tq, jnp.sum(qf * kn[t:t + 1], axis=1, keepdims=True), NEG_INF) for t in range(n_new)]

    m = functools.reduce(jnp.maximum, s_new)
    for n in range(n_blocks):
        for k in range(PAGES_PER_BLOCK):
            pg = PAGES_PER_BLOCK * n + k
            s = jnp.where(keep[n], s_ref[pg], NEG_INF)
            s_ref[pg] = s
            m = jnp.maximum(m, jnp.max(s, axis=1, keepdims=True))
    p_new = [jnp.exp(s - m) for s in s_new]
    l = functools.reduce(lambda a, b: a + b, p_new)
    for pg in range(n_pages):
        e = jnp.exp(s_ref[pg] - m)
        s_ref[pg] = e
        l = l + jnp.sum(e, axis=1, keepdims=True)
    inv = 1.0 / l
    for pg in range(n_pages):
        p_ref[pg] = (s_ref[pg] * inv).astype(BF16)
    pown = jnp.zeros((rows, LANES), F32)
    for t in range(n_new):
        pown = jnp.where(lane == t, p_new[t] * inv, pown)
    pown_ref[...] = pown


def _moba_sample_pv_kernel(n_chunks, pt_ref, p_ref, pown_ref, vn_ref, v_hbm, o_ref, v_buf, sem, acc_ref):
    rows = p_ref.shape[1]
    n_new = vn_ref.shape[0]
    acc_ref[...] = jnp.zeros(acc_ref.shape, F32)

    def compute(slot, c):
        acc = acc_ref[...]
        for k in range(PAGE_CHUNK):
            acc = acc + _dot_nt(p_ref[c * PAGE_CHUNK + k], v_buf[slot, k].astype(BF16))
        acc_ref[...] = acc

    _page_stream(pt_ref, [(v_hbm, v_buf)], sem, n_chunks, compute)

    acc = acc_ref[...]
    pown = pown_ref[...]
    vn = vn_ref[...]
    for t in range(n_new):
        acc = acc + pown[:, t:t + 1] * vn[t:t + 1]
    row_head = lax.broadcasted_iota(jnp.int32, acc.shape, 0) % MOBA_HEADS
    lane_head = lax.broadcasted_iota(jnp.int32, acc.shape, 1) // MOBA_HD
    own = jnp.where(row_head == lane_head, acc, 0.0)
    o_ref[...] = jnp.sum(own.reshape(rows // MOBA_HEADS, MOBA_HEADS, MOBA_W), axis=1).astype(o_ref.dtype)


def _moba_sample(page_table, q, k_new, v_new, cache_k_t, cache_v_t):
    db, n_pages = page_table.shape
    n_chunks = n_pages // PAGE_CHUNK
    t_new = q.shape[1]
    rows = t_new * MOBA_HEADS
    pages = cache_k_t.shape[0]
    kc = cache_k_t.reshape(pages, MOBA_W, PAGE_SIZE)
    vc = cache_v_t.reshape(pages, MOBA_W, PAGE_SIZE)
    head_mask = (jnp.arange(MOBA_HEADS)[:, None] == jnp.arange(MOBA_HEADS)[None, :])
    qbd = jnp.where(head_mask[None, None, :, :, None], q[:, :, None, :, :], 0.0)
    qbd = qbd.reshape(db, rows, MOBA_W).astype(BF16)
    kn = k_new.reshape(db, t_new, MOBA_W)
    vn = v_new.reshape(db, t_new, MOBA_W)
    per_d = lambda *shape: pl.BlockSpec((None,) + shape, lambda d, pt: (d,) + (0,) * len(shape))
    buf = pltpu.VMEM((2, PAGE_CHUNK, MOBA_W, PAGE_SIZE), F32)
    p, pown = pl.pallas_call(
        functools.partial(_moba_sample_scores_kernel, n_chunks),
        grid_spec=pltpu.PrefetchScalarGridSpec(
            num_scalar_prefetch=1, grid=(db,),
            in_specs=[per_d(rows, MOBA_W), per_d(t_new, MOBA_W), pl.BlockSpec(memory_space=pl.ANY)],
            out_specs=(per_d(n_pages, rows, PAGE_SIZE), per_d(rows, LANES)),
            scratch_shapes=[buf, pltpu.SemaphoreType.DMA((2, 1)), pltpu.VMEM((n_pages, rows, PAGE_SIZE), F32)]),
        out_shape=(jax.ShapeDtypeStruct((db, n_pages, rows, PAGE_SIZE), BF16),
                   jax.ShapeDtypeStruct((db, rows, LANES), F32)),
        compiler_params=_cparams("arbitrary"), name="moba_s_scores",
    )(page_table, qbd, kn, kc)
    return pl.pallas_call(
        functools.partial(_moba_sample_pv_kernel, n_chunks),
        grid_spec=pltpu.PrefetchScalarGridSpec(
            num_scalar_prefetch=1, grid=(db,),
            in_specs=[per_d(n_pages, rows, PAGE_SIZE), per_d(rows, LANES), per_d(t_new, MOBA_W),
                      pl.BlockSpec(memory_space=pl.ANY)],
            out_specs=per_d(t_new, MOBA_W),
            scratch_shapes=[buf, pltpu.SemaphoreType.DMA((2, 1)), pltpu.VMEM((rows, MOBA_W), F32)]),
        out_shape=jax.ShapeDtypeStruct((db, t_new, MOBA_W), BF16),
        compiler_params=_cparams("arbitrary"), name="moba_s_pv",
    )(page_table, p, pown, vn, vc)


def kernel(x_prompt, x_sample, cache_mla_latent, cache_mla_krope, cache_moba_k, cache_moba_v,
           state_ffn_conv, page_table, g_attn_norm, w_in, g_qnorm, w_uq, g_kvnorm, w_uk, w_uv,
           w_br_mla, w_br_moba, w_out, g_ffn_norm, w_up, conv_w, conv_b, w_down, g_final):
    b, s, d_model = x_prompt.shape
    db, t_new, _ = x_sample.shape
    assert d_model == D_MODEL and s % ROW_TILE == 0 and (db * t_new) % ROW_TILE == 0
    assert page_table.shape == (db, PAST_LEN // PAGE_SIZE) and ROW_TILE == ATT_TILE == MOBA_BLOCK
    assert PAST_LEN % MOBA_BLOCK == 0 and ROW_TILE % t_new == 0

    pw = _prep_proj_weights(g_attn_norm, w_in, g_qnorm, w_uq, g_kvnorm, w_uk)
    tw = _prep_tail_weights(w_br_mla, w_br_moba, w_out, g_ffn_norm, w_up, conv_w, conv_b, w_down, g_final)
    wuv = _prep_wuv(w_uv)

    xp = x_prompt.reshape(b * s, D_MODEL)
    tabs_p = _rope_tables(jnp.arange(s, dtype=jnp.int32))
    ckv, ckvb, krt, kr8, kt, vt, qm, qa, qr, ga, gb = _proj(xp, tabs_p, s // ROW_TILE, pw)
    o_mla = _mla_prompt(qa, qr, ckvb, kr8, wuv, b, s)
    o_moba = _moba_prompt(qm, kt, vt, b, s)
    y_p, conv_p = _tail(xp, o_mla, o_moba, ga, gb, tw, s // ROW_TILE)
    c_p = ckv.reshape(b, s, KV_LORA)
    kr_p = jnp.transpose(krt, (0, 2, 1))
    k_p = jnp.transpose(kt.reshape(b, MOBA_HEADS, MOBA_HD, s), (0, 3, 1, 2))
    v_p = jnp.transpose(vt.reshape(b, MOBA_HEADS, MOBA_HD, s), (0, 3, 1, 2))

    n_s = db * t_new
    xs = x_sample.reshape(n_s, D_MODEL)
    pos_s = jnp.tile(PAST_LEN + jnp.arange(t_new, dtype=jnp.int32), db)
    tabs_s = _rope_tables(pos_s)
    ckv_s, _, krt_s, _, kt_s, vt_s, qm_s, qa_s, qr_s, ga_s, gb_s = _proj(xs, tabs_s, n_s // ROW_TILE, pw)
    c_s = ckv_s.reshape(db, t_new, KV_LORA)
    kr_s = krt_s[0].T.reshape(db, t_new, ROPE_DIM)
    k_s = kt_s[0].T.reshape(db, t_new, MOBA_HEADS, MOBA_HD)
    v_s = vt_s[0].T.reshape(db, t_new, MOBA_HEADS, MOBA_HD)
    qa_rows = jnp.transpose(qa_s[0].reshape(MLA_HEADS, db, t_new, KV_LORA), (1, 0, 2, 3))
    qa_rows = qa_rows.reshape(db, MLA_HEADS * t_new, KV_LORA)
    qr_rows = jnp.transpose(qr_s.reshape(db, t_new, MLA_HEADS, ROPE_DIM), (0, 2, 1, 3))
    qr_rows = qr_rows.reshape(db, MLA_HEADS * t_new, ROPE_DIM)
    o_lat_s = _mla_sample(page_table, qa_rows, qr_rows, c_s, kr_s, cache_mla_latent,
                          jnp.transpose(cache_mla_krope, (0, 2, 1)))
    o_lat_s = jnp.transpose(o_lat_s.reshape(db, MLA_HEADS, t_new, KV_LORA), (1, 0, 2, 3))
    o_lat_s = o_lat_s.reshape(MLA_HEADS, n_s, KV_LORA)
    o_moba_s = _moba_sample(page_table, qm_s.reshape(db, t_new, MOBA_HEADS, MOBA_HD), k_s, v_s,
                            jnp.transpose(cache_moba_k, (0, 2, 3, 1)), jnp.transpose(cache_moba_v, (0, 2, 3, 1)))
    hist = jnp.concatenate([state_ffn_conv, jnp.zeros((db, t_new - (CONV_W - 1), 2 * D_FF), F32)], axis=1)
    y_s, u_s = _tail(xs, o_lat_s, o_moba_s.reshape(n_s, MOBA_W), ga_s, gb_s, tw, 0,
                     hist=hist.reshape(n_s, -1), seq_rows=t_new, wuv=wuv, tm=SAMPLE_TILE)
    conv_s = u_s.reshape(db, t_new, 2 * D_FF)[:, t_new - (CONV_W - 1):]

    return (y_p.reshape(b, s, D_MODEL), y_s.reshape(db, t_new, D_MODEL), c_p, kr_p, k_p, v_p, conv_p,
            c_s, kr_s, k_s, v_s, conv_s)
```

```python
import functools

import jax
import jax.numpy as jnp
import numpy as np
from jax import lax
from jax.experimental import pallas as pl
from jax.experimental.pallas import tpu as pltpu

D_MODEL = 1024
PAST_LEN = 8192
PAGE_SIZE = 128

MLA_HEADS = 8
Q_LORA = 512
KV_LORA = 256
NOPE_DIM = 64
ROPE_DIM = 32
V_DIM = 64
MLA_QK = NOPE_DIM + ROPE_DIM
MLA_W = MLA_HEADS * V_DIM
MLA_SCALE = MLA_QK ** -0.5
MLA_THETA = 10000.0

MOBA_HEADS = 8
MOBA_HD = 64
MOBA_W = MOBA_HEADS * MOBA_HD
MOBA_ROT = MOBA_HD // 4
MOBA_BLOCK = 256
MOBA_TOPK = 3
MOBA_SCALE = MOBA_HD ** -0.5
ROPE_THETA = 500000.0

D_FF = 2816
CONV_W = 3
EPS = 1e-6

LANES = 128
ROW_TILE = 256
VMEM_LIMIT = 56 * 1024 * 1024
NEG_INF = float("-inf")

BF16 = jnp.bfloat16
F32 = jnp.float32


def _cparams(*sem):
    return pltpu.CompilerParams(dimension_semantics=sem, vmem_limit_bytes=VMEM_LIMIT)


def _dot(a, b):
    return jnp.dot(a, b, preferred_element_type=F32)


def _dot_nt(a, b):
    return lax.dot_general(a, b, (((1,), (1,)), ((), ())), preferred_element_type=F32)


def _rms(x, g):
    return x * lax.rsqrt(jnp.mean(x * x, axis=-1, keepdims=True) + EPS) * g


def _angles(pos, d, theta):
    half = d // 2
    inv = 1.0 / (theta ** (jnp.arange(half, dtype=F32) * (2.0 / d)))
    ang = pos.astype(F32)[:, None] * inv[None, :]
    return jnp.cos(ang), jnp.sin(ang)


def _row_table(cos, sin, group, lanes):
    half = cos.shape[1]
    p = cos.shape[0]
    lane = np.arange(lanes) % group
    first = lane < half
    second = (lane >= half) & (lane < 2 * half)
    idx = np.where(first, lane, np.where(second, lane - half, 0))
    c = jnp.where((first | second)[None, :], cos[:, idx], 1.0)
    sa = jnp.where(first[None, :], -sin[:, idx], 0.0)
    sb = jnp.where(second[None, :], sin[:, idx], 0.0)
    del p
    return jnp.concatenate([c, sa, sb], axis=1).astype(F32)


def _rope_tables(pos):
    cm, sm = _angles(pos, MOBA_ROT, ROPE_THETA)
    cr, sr = _angles(pos, ROPE_DIM, MLA_THETA)
    t_qm = _row_table(cm, sm, MOBA_HD, LANES)
    t_qr = _row_table(cr, sr, ROPE_DIM, LANES)
    t_kt = jnp.concatenate([cm.T, sm.T], axis=0)
    t_rt = jnp.concatenate([cr.T, sr.T], axis=0)
    return t_qm, t_qr, t_kt, t_rt


def _rope_rows(x, tab, half):
    c, sa, sb = tab[:, :LANES], tab[:, LANES:2 * LANES], tab[:, 2 * LANES:]
    outs = []
    for k in range(x.shape[1] // LANES):
        xk = x[:, k * LANES:(k + 1) * LANES]
        up = pltpu.roll(xk, LANES - half, axis=1)
        dn = pltpu.roll(xk, half, axis=1)
        outs.append(xk * c + up * sa + dn * sb)
    return jnp.concatenate(outs, axis=1) if len(outs) > 1 else outs[0]


W_T_ROWS = 2 * MOBA_W + ROPE_DIM
MLA_KW = KV_LORA + LANES


def _proj_kernel(x_ref, g_ref, wrow_ref, wt_ref, gq_ref, wuq_ref, gkv_ref, wuk_ref,
                 tqm_ref, tqr_ref, tkt_ref, trt_ref,
                 ckv_ref, kcat_ref, ckvt_ref, krt_ref, kt_ref, vt_ref,
                 qm_ref, qcat_ref, ga_ref, gb_ref):
    x = x_ref[...]
    h = _rms(x, g_ref[...]).astype(BF16)

    o = 0
    q_lat = _dot(h, wrow_ref[:, o:o + Q_LORA]); o += Q_LORA
    kv_lat = _dot(h, wrow_ref[:, o:o + KV_LORA]); o += KV_LORA
    q_m = _dot(h, wrow_ref[:, o:o + MOBA_W]); o += MOBA_W
    k_r = _dot(h, wrow_ref[:, o:o + LANES]); o += LANES
    ga_ref[...] = jax.nn.sigmoid(_dot(h, wrow_ref[:, o:o + D_MODEL])); o += D_MODEL
    gb_ref[...] = jax.nn.sigmoid(_dot(h, wrow_ref[:, o:o + D_MODEL]))

    ckv = _rms(kv_lat, gkv_ref[...])
    ckv_ref[...] = ckv
    kcat_ref[:, :KV_LORA] = ckv.astype(BF16)
    kcat_ref[:, KV_LORA:] = _rope_rows(k_r, tqr_ref[...], ROPE_DIM // 2).astype(BF16)
    ckvt_ref[...] = ckv.T.astype(BF16)

    qm_ref[...] = (_rope_rows(q_m, tqm_ref[...], MOBA_ROT // 2) * MOBA_SCALE).astype(BF16)

    qn = _rms(q_lat, gq_ref[...]).astype(BF16)
    q = _dot(qn, wuq_ref[...])
    q_nope = q[:, :MLA_HEADS * NOPE_DIM].astype(BF16)
    q_rope = _rope_rows(q[:, MLA_HEADS * NOPE_DIM:], tqr_ref[...], ROPE_DIM // 2)
    lane = lax.broadcasted_iota(jnp.int32, (q_rope.shape[0], LANES), 1)
    heads_per_tile = LANES // ROPE_DIM
    for j in range(MLA_HEADS // 2):
        qa2 = _dot(q_nope[:, j * LANES:(j + 1) * LANES], wuk_ref[j])
        qcat_ref[2 * j, :, :KV_LORA] = qa2[:, :KV_LORA].astype(BF16)
        qcat_ref[2 * j + 1, :, :KV_LORA] = qa2[:, KV_LORA:].astype(BF16)
    for hd in range(MLA_HEADS):
        chunk = q_rope[:, (hd // heads_per_tile) * LANES:(hd // heads_per_tile + 1) * LANES]
        shift = (hd % heads_per_tile) * ROPE_DIM
        if shift:
            chunk = pltpu.roll(chunk, LANES - shift, axis=1)
        qcat_ref[hd, :, KV_LORA:] = jnp.where(lane < ROPE_DIM, chunk, 0.0).astype(BF16)

    yt = _dot_nt(wt_ref[...], h)
    tkt = tkt_ref[...]
    cm, sm = tkt[:MOBA_ROT // 2], tkt[MOBA_ROT // 2:]
    hr = MOBA_ROT // 2
    for hd in range(MOBA_HEADS):
        r0 = hd * MOBA_HD
        x1 = yt[r0:r0 + hr]
        x2 = yt[r0 + hr:r0 + 2 * hr]
        kt_ref[r0:r0 + hr, :] = x1 * cm - x2 * sm
        kt_ref[r0 + hr:r0 + 2 * hr, :] = x2 * cm + x1 * sm
        kt_ref[r0 + 2 * hr:r0 + MOBA_HD, :] = yt[r0 + 2 * hr:r0 + MOBA_HD]
    vt_ref[...] = yt[MOBA_W:2 * MOBA_W]
    trt = trt_ref[...]
    cr, sr = trt[:ROPE_DIM // 2], trt[ROPE_DIM // 2:]
    y1 = yt[2 * MOBA_W:2 * MOBA_W + ROPE_DIM // 2]
    y2 = yt[2 * MOBA_W + ROPE_DIM // 2:]
    krt_ref[...] = jnp.concatenate([y1 * cr - y2 * sr, y2 * cr + y1 * sr], axis=0)


def _proj(x2d, tables, tiles_per_seq, wts):
    n = x2d.shape[0]
    tm = ROW_TILE
    nseq = n // (tm * tiles_per_seq)
    seq_len = tm * tiles_per_seq
    t_qm, t_qr, t_kt, t_rt = tables
    g_attn, w_row, w_t, g_q, w_uq, g_kv, w_ukp = wts

    def full(a):
        nd = a.ndim
        return pl.BlockSpec(a.shape, lambda i: (0,) * nd)

    row = lambda w: pl.BlockSpec((tm, w), lambda i: (i, 0))
    tab_row = lambda w: pl.BlockSpec((tm, w), lambda i: (i % tiles_per_seq, 0))
    tab_col = lambda r: pl.BlockSpec((r, tm), lambda i: (0, i % tiles_per_seq))
    col = lambda r: pl.BlockSpec((None, r, tm), lambda i: (i // tiles_per_seq, 0, i % tiles_per_seq))

    out_shape = (
        jax.ShapeDtypeStruct((n, KV_LORA), F32),
        jax.ShapeDtypeStruct((n, MLA_KW), BF16),
        jax.ShapeDtypeStruct((nseq, tiles_per_seq, KV_LORA, tm), BF16),
        jax.ShapeDtypeStruct((nseq, ROPE_DIM, seq_len), F32),
        jax.ShapeDtypeStruct((nseq, MOBA_W, seq_len), F32),
        jax.ShapeDtypeStruct((nseq, MOBA_W, seq_len), F32),
        jax.ShapeDtypeStruct((n, MOBA_W), BF16),
        jax.ShapeDtypeStruct((nseq, MLA_HEADS, seq_len, MLA_KW), BF16),
        jax.ShapeDtypeStruct((n, D_MODEL), F32),
        jax.ShapeDtypeStruct((n, D_MODEL), F32),
    )
    out_specs = (
        row(KV_LORA), row(MLA_KW),
        pl.BlockSpec((None, None, KV_LORA, tm), lambda i: (i // tiles_per_seq, i % tiles_per_seq, 0, 0)),
        col(ROPE_DIM), col(MOBA_W), col(MOBA_W),
        row(MOBA_W),
        pl.BlockSpec((None, MLA_HEADS, tm, MLA_KW), lambda i: (i // tiles_per_seq, 0, i % tiles_per_seq, 0)),
        row(D_MODEL), row(D_MODEL),
    )
    in_specs = [row(D_MODEL), full(g_attn), full(w_row), full(w_t), full(g_q), full(w_uq), full(g_kv),
                full(w_ukp), tab_row(3 * LANES), tab_row(3 * LANES), tab_col(MOBA_ROT), tab_col(ROPE_DIM)]
    return pl.pallas_call(
        _proj_kernel, grid=(n // tm,), in_specs=in_specs, out_specs=out_specs, out_shape=out_shape,
        compiler_params=_cparams("arbitrary"), name="proj",
    )(x2d, g_attn, w_row, w_t, g_q, w_uq, g_kv, w_ukp, t_qm, t_qr, t_kt, t_rt)


def _prep_proj_weights(g_attn_norm, w_in, g_qnorm, w_uq, g_kvnorm, w_uk):
    o_q, o_kv, o_kr = 0, Q_LORA, Q_LORA + KV_LORA
    o_qm = o_kr + ROPE_DIM
    o_km, o_vm = o_qm + MOBA_W, o_qm + 2 * MOBA_W
    o_ga = o_vm + MOBA_W
    w_kr = jnp.pad(w_in[:, o_kr:o_qm], ((0, 0), (0, LANES - ROPE_DIM)))
    w_row = jnp.concatenate([w_in[:, o_q:o_kr], w_in[:, o_qm:o_km], w_kr, w_in[:, o_ga:]], axis=1).astype(BF16)
    w_t = jnp.concatenate([w_in[:, o_km:o_vm], w_in[:, o_vm:o_ga], w_in[:, o_kr:o_qm]], axis=1).T.astype(BF16)
    wq = w_uq.reshape(Q_LORA, MLA_HEADS, MLA_QK)
    w_uq_p = jnp.concatenate([wq[:, :, :NOPE_DIM].reshape(Q_LORA, -1),
                              wq[:, :, NOPE_DIM:].reshape(Q_LORA, -1)], axis=1).astype(BF16)
    wk = jnp.transpose(w_uk, (1, 2, 0)).astype(BF16)
    z = jnp.zeros((NOPE_DIM, KV_LORA), BF16)
    w_ukp = jnp.stack([
        jnp.concatenate([jnp.concatenate([wk[2 * j], z], axis=1),
                         jnp.concatenate([z, wk[2 * j + 1]], axis=1)], axis=0)
        for j in range(MLA_HEADS // 2)])
    return (g_attn_norm.reshape(1, -1), w_row, w_t, g_qnorm.reshape(1, -1), w_uq_p,
            g_kvnorm.reshape(1, -1), w_ukp)


ATT_TILE = 256


def _mla_prompt_kernel(q_ref, kcat_ref, ckvt_ref, wuvt_ref, o_ref, m_ref, l_ref, acc_ref):
    i = pl.program_id(1)
    t = ATT_TILE
    cols = MLA_HEADS * t
    q = q_ref[...].reshape(cols, MLA_KW)
    m_ref[...] = jnp.full(m_ref.shape, NEG_INF, F32)
    l_ref[...] = jnp.zeros(l_ref.shape, F32)
    acc_ref[...] = jnp.zeros(acc_ref.shape, F32)

    def step(j, mask):
        kc = kcat_ref[pl.ds(pl.multiple_of(j * t, t), t), :]
        st = _dot_nt(kc, q) * MLA_SCALE
        if mask is not None:
            st = jnp.where(mask, st, NEG_INF)
        m_old = m_ref[...]
        m_new = jnp.maximum(m_old, jnp.max(st, axis=0, keepdims=True))
        alpha = jnp.exp(m_old - m_new)
        p = jnp.exp(st - m_new)
        l_ref[...] = alpha * l_ref[...] + jnp.sum(p, axis=0, keepdims=True)
        acc_ref[...] = alpha * acc_ref[...] + _dot(ckvt_ref[j], p.astype(BF16))
        m_ref[...] = m_new

    def body(j, carry):
        step(j, None)
        return carry

    lax.fori_loop(0, i, body, 0)
    kpos = lax.broadcasted_iota(jnp.int32, (t, cols), 0)
    qpos = lax.broadcasted_iota(jnp.int32, (t, cols), 1) % t
    step(i, kpos <= qpos)

    o_lat = (acc_ref[...] / l_ref[...]).astype(BF16)
    o_t = jnp.concatenate([_dot(wuvt_ref[h], o_lat[:, h * t:(h + 1) * t]) for h in range(MLA_HEADS)], axis=0)
    o_ref[...] = o_t.T.astype(BF16)


def _prep_wuv(w_uv):
    w = jnp.transpose(w_uv, (1, 0, 2)).astype(BF16)
    z = jnp.zeros_like(w)
    even = jnp.concatenate([w, z], axis=2)
    odd = jnp.concatenate([z, w], axis=2)
    is_even = (jnp.arange(MLA_HEADS) % 2 == 0)[:, None, None]
    return jnp.where(is_even, even, odd)


def _prep_wuvt(w_uv):
    return jnp.transpose(w_uv, (1, 2, 0)).astype(BF16)


def _mla_prompt(qcat, kcat, ckvt, wuvt, b, s):
    t = ATT_TILE
    nq = s // t
    cols = MLA_HEADS * t
    return pl.pallas_call(
        _mla_prompt_kernel, grid=(b, nq),
        in_specs=[
            pl.BlockSpec((None, MLA_HEADS, t, MLA_KW), lambda bi, i: (bi, 0, i, 0)),
            pl.BlockSpec((s, MLA_KW), lambda bi, i: (bi, 0)),
            pl.BlockSpec((None, nq, KV_LORA, t), lambda bi, i: (bi, 0, 0, 0)),
            pl.BlockSpec(wuvt.shape, lambda bi, i: (0, 0, 0)),
        ],
        out_specs=pl.BlockSpec((t, MLA_W), lambda bi, i: (bi * nq + i, 0)),
        out_shape=jax.ShapeDtypeStruct((b * s, MLA_W), BF16),
        scratch_shapes=[pltpu.VMEM((1, cols), F32), pltpu.VMEM((1, cols), F32), pltpu.VMEM((KV_LORA, cols), F32)],
        compiler_params=_cparams("arbitrary", "arbitrary"), name="mla_p",
    )(qcat, kcat, ckvt, wuvt)


def _moba_attend(nb, q, krow_ref, vbf_ref, o_ref):
    t = MOBA_BLOCK
    nk = nb * t
    lane = lax.broadcasted_iota(jnp.int32, (t, LANES), 1)
    causal = lax.broadcasted_iota(jnp.int32, (t, t), 0) <= lax.broadcasted_iota(jnp.int32, (t, t), 1)
    n_past = nb - 1
    halves = []
    for half in range(2):
        qh = jnp.where((lane >= MOBA_HD) == bool(half), q, jnp.zeros_like(q))
        st = _dot_nt(krow_ref[0:nk, :], qh)
        blks = [st[n * t:(n + 1) * t] for n in range(nb)]
        blks[-1] = jnp.where(causal, blks[-1], NEG_INF)
        bmax = [jnp.max(b, axis=0, keepdims=True) for b in blks]
        mx = bmax[-1]
        if n_past > MOBA_TOPK:
            bsum = [jnp.sum(b, axis=0, keepdims=True) for b in blks[:-1]]
            keep = []
            for n in range(n_past):
                rank = jnp.zeros((1, t), F32)
                for m in range(n_past):
                    if m != n:
                        beats = (bsum[m] >= bsum[n]) if m < n else (bsum[m] > bsum[n])
                        rank = rank + beats.astype(F32)
                keep.append(rank < MOBA_TOPK)
                mx = jnp.maximum(mx, jnp.where(keep[n], bmax[n], NEG_INF))
        else:
            keep = [None] * n_past
            for n in range(n_past):
                mx = jnp.maximum(mx, bmax[n])
        l = jnp.zeros((1, t), F32)
        ps = []
        for n in range(nb):
            p = jnp.exp(blks[n] - mx)
            if n < n_past and keep[n] is not None:
                p = jnp.where(keep[n], p, 0.0)
            l = l + jnp.sum(p, axis=0, keepdims=True)
            ps.append(p.astype(BF16))
        pt = jnp.concatenate(ps, axis=0) if nb > 1 else ps[0]
        halves.append(_dot(vbf_ref[:, 0:nk], pt) / l)
    sub = lax.broadcasted_iota(jnp.int32, (LANES, t), 0)
    o_ref[...] = jnp.where(sub < MOBA_HD, halves[0], halves[1]).T.astype(BF16)


def _moba_prompt_kernel(nblk, q_ref, kt_ref, vt_ref, o_ref, krow_ref, vbf_ref):
    i = pl.program_id(2)

    @pl.when(i == 0)
    def _():
        krow_ref[...] = kt_ref[...].T.astype(BF16)
        vbf_ref[...] = vt_ref[...].astype(BF16)

    q = q_ref[...]
    for nb in range(1, nblk + 1):
        pl.when(i == nb - 1)(functools.partial(_moba_attend, nb, q, krow_ref, vbf_ref, o_ref))


def _moba_prompt(qm, kt, vt, b, s):
    t = MOBA_BLOCK
    nq = s // t
    npair = MOBA_W // LANES
    return pl.pallas_call(
        functools.partial(_moba_prompt_kernel, nq), grid=(b, npair, nq),
        in_specs=[
            pl.BlockSpec((t, LANES), lambda bi, j, i: (bi * nq + i, j)),
            pl.BlockSpec((None, LANES, s), lambda bi, j, i: (bi, j, 0)),
            pl.BlockSpec((None, LANES, s), lambda bi, j, i: (bi, j, 0)),
        ],
        out_specs=pl.BlockSpec((t, LANES), lambda bi, j, i: (bi * nq + i, j)),
        out_shape=jax.ShapeDtypeStruct((b * s, MOBA_W), BF16),
        scratch_shapes=[pltpu.VMEM((s, LANES), BF16), pltpu.VMEM((LANES, s), BF16)],
        compiler_params=_cparams("arbitrary", "arbitrary", "arbitrary"), name="moba_p",
    )(qm, kt, vt)


FF_CHUNK = D_FF // 2
HIST_ROWS = 8
SAMPLE_TILE = 128


def _tail_kernel(seq_tiles, seq_rows, *refs):
    per_row_hist = seq_tiles == 0
    if per_row_hist:
        (x_ref, olat_ref, ob_ref, ga_ref, gb_ref, hist_ref, wuv_ref, wa_ref, wb_ref, wo_ref, gf_ref,
         wup_ref, cw_ref, cb_ref, wdn_ref, gfin_ref, y_ref, u_ref, ue_ref) = refs
        om = jnp.concatenate(
            [_dot(olat_ref[2 * j], wuv_ref[2 * j]) + _dot(olat_ref[2 * j + 1], wuv_ref[2 * j + 1])
             for j in range(MLA_HEADS // 2)], axis=1).astype(BF16)
    else:
        (x_ref, om_ref, ob_ref, ga_ref, gb_ref, wa_ref, wb_ref, wo_ref, gf_ref, wup_ref,
         cw_ref, cb_ref, wdn_ref, gfin_ref, y_ref, conv_ref, ue_ref, carry_ref) = refs
        om = om_ref[...]
    i = pl.program_id(0)
    tm = x_ref.shape[0]
    cw = FF_CHUNK

    a = _dot(om, wa_ref[...])
    b = _dot(ob_ref[...], wb_ref[...])
    mg = (ga_ref[...] * a + gb_ref[...] * b).astype(BF16)
    x1 = x_ref[...] + _dot(mg, wo_ref[...])
    hn = _rms(x1, gf_ref[...]).astype(BF16)

    if per_row_hist:
        tpos = lax.broadcasted_iota(jnp.int32, (tm, cw), 0) % seq_rows
        ue_ref[0:HIST_ROWS, :] = jnp.zeros((HIST_ROWS, cw), F32)
    else:
        first = (i % seq_tiles) == 0

        @pl.when(i == 0)
        def _():
            carry_ref[...] = jnp.zeros(carry_ref.shape, F32)

    acc = x1
    for c in range(D_FF // cw):
        halves = []
        for half in range(2):
            c0 = half * D_FF + c * cw
            u = _dot(hn, wup_ref[:, c0:c0 + cw])
            if not per_row_hist:
                ue_ref[0:HIST_ROWS, :] = jnp.where(first, 0.0, carry_ref[:, c0:c0 + cw])
            ue_ref[HIST_ROWS:, :] = u
            u1 = ue_ref[HIST_ROWS - 1:HIST_ROWS - 1 + tm, :]
            u2 = ue_ref[HIST_ROWS - 2:HIST_ROWS - 2 + tm, :]
            if per_row_hist:
                hh = hist_ref[:, c0:c0 + cw]
                u1 = jnp.where(tpos < 1, pltpu.roll(hh, tm - 1, axis=0), u1)
                u2 = jnp.where(tpos < 2, hh, u2)
                u_ref[:, c0:c0 + cw] = u
            else:
                carry_ref[:, c0:c0 + cw] = u[tm - HIST_ROWS:, :]
                conv_ref[:, c0:c0 + cw] = u[tm - (CONV_W - 1):, :]
            halves.append(cb_ref[:, c0:c0 + cw] + cw_ref[0:1, c0:c0 + cw] * u2
                          + cw_ref[1:2, c0:c0 + cw] * u1 + cw_ref[2:3, c0:c0 + cw] * u)
        act = (jax.nn.silu(halves[0]) * halves[1]).astype(BF16)
        acc = acc + _dot(act, wdn_ref[c * cw:(c + 1) * cw, :])
    y_ref[...] = _rms(acc, gfin_ref[...])


def _tail(x2d, o_mla, o_moba, ga, gb, wts, seq_tiles, hist=None, seq_rows=0, wuv=None, tm=ROW_TILE):
    n = x2d.shape[0]
    row = lambda w: pl.BlockSpec((tm, w), lambda i: (i, 0))

    def const(a):
        nd = a.ndim
        return pl.BlockSpec(a.shape, lambda i: (0,) * nd, pipeline_mode=pl.Buffered(1))

    acts = [x2d, o_mla, o_moba, ga, gb]
    act_specs = [row(D_MODEL), row(MLA_W), row(MOBA_W), row(D_MODEL), row(D_MODEL)]
    scratch = [pltpu.VMEM((tm + HIST_ROWS, FF_CHUNK), F32)]
    if seq_tiles == 0:
        acts.append(hist)
        act_specs[1] = pl.BlockSpec((MLA_HEADS, tm, KV_LORA), lambda i: (0, i, 0))
        act_specs.append(row(2 * D_FF))
        wts = (wuv,) + tuple(wts)
        out_shape = (jax.ShapeDtypeStruct((n, D_MODEL), F32), jax.ShapeDtypeStruct((n, 2 * D_FF), F32))
        out_specs = (row(D_MODEL), row(2 * D_FF))
    else:
        nseq = n // (tm * seq_tiles)
        out_shape = (jax.ShapeDtypeStruct((n, D_MODEL), F32),
                     jax.ShapeDtypeStruct((nseq, CONV_W - 1, 2 * D_FF), F32))
        out_specs = (row(D_MODEL), pl.BlockSpec((None, CONV_W - 1, 2 * D_FF), lambda i: (i // seq_tiles, 0, 0)))
        scratch.append(pltpu.VMEM((HIST_ROWS, 2 * D_FF), F32))
    return pl.pallas_call(
        functools.partial(_tail_kernel, seq_tiles, seq_rows), grid=(n // tm,),
        in_specs=act_specs + [const(w) for w in wts], out_specs=out_specs, out_shape=out_shape,
        scratch_shapes=scratch, compiler_params=_cparams("arbitrary"), name="tail",
    )(*acts, *wts)


def _prep_tail_weights(w_br_mla, w_br_moba, w_out, g_ffn_norm, w_up, conv_w, conv_b, w_down, g_final):
    return (w_br_mla.astype(BF16), w_br_moba.astype(BF16), w_out.astype(BF16), g_ffn_norm.reshape(1, -1),
            w_up.astype(BF16), conv_w, conv_b.reshape(1, -1), w_down.astype(BF16), g_final.reshape(1, -1))


PAGE_CHUNK = 32


def _page_stream(pt_ref, hbm_bufs, sem, n_chunks, compute):
    d = pl.program_id(0)
    total = pl.num_programs(0) * n_chunks

    def copies(g, slot):
        dd = g // n_chunks
        c = g % n_chunks
        out = []
        for k in range(PAGE_CHUNK):
            page = pt_ref[dd, c * PAGE_CHUNK + k]
            for idx, (hbm, buf) in enumerate(hbm_bufs):
                out.append(pltpu.make_async_copy(hbm.at[page], buf.at[slot, k], sem.at[slot, idx]))
        return out

    @pl.when(d == 0)
    def _():
        for cp in copies(0, 0):
            cp.start()

    def body(c, carry):
        g = d * n_chunks + c
        slot = g % 2

        @pl.when(g + 1 < total)
        def _():
            for cp in copies(g + 1, 1 - slot):
                cp.start()

        for cp in copies(g, slot):
            cp.wait()
        compute(slot, c)
        return carry

    lax.fori_loop(0, n_chunks, body, 0)


def _mla_sample_kernel(n_chunks, pt_ref, qa_ref, qr_ref, cn_ref, krn_ref, lat_hbm, kr_hbm, o_ref,
                       lat_buf, kr_buf, sem, m_ref, l_ref, acc_ref):
    qa = qa_ref[...]
    qr = qr_ref[...]
    rows = qa.shape[0]
    n_new = cn_ref.shape[0]
    m_ref[...] = jnp.full(m_ref.shape, NEG_INF, F32)
    l_ref[...] = jnp.zeros(l_ref.shape, F32)
    acc_ref[...] = jnp.zeros(acc_ref.shape, F32)

    def compute(slot, c):
        del c
        cb = lat_buf[slot].reshape(PAGE_CHUNK * PAGE_SIZE, KV_LORA).astype(BF16)
        s_rope = jnp.concatenate([_dot(qr, kr_buf[slot, k].astype(BF16)) for k in range(PAGE_CHUNK)], axis=1)
        s = (_dot_nt(qa, cb) + s_rope) * MLA_SCALE
        m_old = m_ref[...]
        m_new = jnp.maximum(m_old, jnp.max(s, axis=1, keepdims=True))
        alpha = jnp.exp(m_old - m_new)
        p = jnp.exp(s - m_new)
        l_ref[...] = alpha * l_ref[...] + jnp.sum(p, axis=1, keepdims=True)
        acc_ref[...] = alpha * acc_ref[...] + _dot(p.astype(BF16), cb)
        m_ref[...] = m_new

    _page_stream(pt_ref, [(lat_hbm, lat_buf), (kr_hbm, kr_buf)], sem, n_chunks, compute)

    qaf = qa.astype(F32)
    qrf = qr.astype(F32)
    cn = cn_ref[...]
    krn = krn_ref[...]
    tq = lax.broadcasted_iota(jnp.int32, (rows, 1), 0) % n_new
    s_new = []
    for t in range(n_new):
        st = (jnp.sum(qaf * cn[t:t + 1], axis=1, keepdims=True)
              + jnp.sum(qrf * krn[t:t + 1], axis=1, keepdims=True)) * MLA_SCALE
        s_new.append(jnp.where(t <= tq, st, NEG_INF))
    m_old = m_ref[...]
    m_new = functools.reduce(jnp.maximum, s_new, m_old)
    alpha = jnp.exp(m_old - m_new)
    l = alpha * l_ref[...]
    acc = alpha * acc_ref[...]
    for t in range(n_new):
        p = jnp.exp(s_new[t] - m_new)
        l = l + p
        acc = acc + p * cn[t:t + 1]
    o_ref[...] = (acc / l).astype(o_ref.dtype)


def _mla_sample(page_table, qa_rows, qr_rows, c_new, kr_new, cache_lat, cache_kr_t):
    db, n_pages = page_table.shape
    n_chunks = n_pages // PAGE_CHUNK
    rows = qa_rows.shape[1]
    t_new = c_new.shape[1]
    per_d = lambda *shape: pl.BlockSpec((None,) + shape, lambda d, pt: (d,) + (0,) * len(shape))
    grid_spec = pltpu.PrefetchScalarGridSpec(
        num_scalar_prefetch=1, grid=(db,),
        in_specs=[per_d(rows, KV_LORA), per_d(rows, ROPE_DIM), per_d(t_new, KV_LORA), per_d(t_new, ROPE_DIM),
                  pl.BlockSpec(memory_space=pl.ANY), pl.BlockSpec(memory_space=pl.ANY)],
        out_specs=per_d(rows, KV_LORA),
        scratch_shapes=[pltpu.VMEM((2, PAGE_CHUNK, PAGE_SIZE, KV_LORA), F32),
                        pltpu.VMEM((2, PAGE_CHUNK, ROPE_DIM, PAGE_SIZE), F32),
                        pltpu.SemaphoreType.DMA((2, 2)),
                        pltpu.VMEM((rows, 1), F32), pltpu.VMEM((rows, 1), F32), pltpu.VMEM((rows, KV_LORA), F32)])
    return pl.pallas_call(
        functools.partial(_mla_sample_kernel, n_chunks), grid_spec=grid_spec,
        out_shape=jax.ShapeDtypeStruct((db, rows, KV_LORA), BF16),
        compiler_params=_cparams("arbitrary"), name="mla_s",
    )(page_table, qa_rows, qr_rows, c_new, kr_new, cache_lat, cache_kr_t)


PAGES_PER_BLOCK = MOBA_BLOCK // PAGE_SIZE


def _moba_sample_scores_kernel(n_chunks, pt_ref, q_ref, kn_ref, k_hbm, p_ref, pown_ref, k_buf, sem, s_ref):
    q = q_ref[...]
    rows = q.shape[0]
    n_new = kn_ref.shape[0]
    n_pages = n_chunks * PAGE_CHUNK
    n_blocks = n_pages // PAGES_PER_BLOCK

    def compute(slot, c):
        for k in range(PAGE_CHUNK):
            s_ref[c * PAGE_CHUNK + k] = _dot(q, k_buf[slot, k].astype(BF16))

    _page_stream(pt_ref, [(k_hbm, k_buf)], sem, n_chunks, compute)

    lane = lax.broadcasted_iota(jnp.int32, (rows, LANES), 1)
    bsum = [sum(jnp.sum(s_ref[PAGES_PER_BLOCK * n + k], axis=1, keepdims=True) for k in range(PAGES_PER_BLOCK))
            for n in range(n_blocks)]
    bs = jnp.full((rows, LANES), NEG_INF, F32)
    for n in range(n_blocks):
        bs = jnp.where(lane == n, bsum[n], bs)
    keep = []
    for n in range(n_blocks):
        beats = jnp.logical_or(bs > bsum[n], jnp.logical_and(bs == bsum[n], lane < n))
        keep.append(jnp.sum(beats.astype(F32), axis=1, keepdims=True) < MOBA_TOPK)

    qf = q.astype(F32)
    kn = kn_ref[...]
    tq = lax.broadcasted_iota(jnp.int32, (rows, 1), 0) // MOBA_HEADS
    s_new = [jnp.where(t <= tq, jnp.sum(qf * kn[t:t + 1], axis=1, keepdims=True), NEG_INF) for t in range(n_new)]

    m = functools.reduce(jnp.maximum, s_new)
    for n in range(n_blocks):
        for k in range(PAGES_PER_BLOCK):
            pg = PAGES_PER_BLOCK * n + k
            s = jnp.where(keep[n], s_ref[pg], NEG_INF)
            s_ref[pg] = s
            m = jnp.maximum(m, jnp.max(s, axis=1, keepdims=True))
    p_new = [jnp.exp(s - m) for s in s_new]
    l = functools.reduce(lambda a, b: a + b, p_new)
    for pg in range(n_pages):
        e = jnp.exp(s_ref[pg] - m)
        s_ref[pg] = e
        l = l + jnp.sum(e, axis=1, keepdims=True)
    inv = 1.0 / l
    for pg in range(n_pages):
        p_ref[pg] = (s_ref[pg] * inv).astype(BF16)
    pown = jnp.zeros((rows, LANES), F32)
    for t in range(n_new):
        pown = jnp.where(lane == t, p_new[t] * inv, pown)
    pown_ref[...] = pown


def _moba_sample_pv_kernel(n_chunks, pt_ref, p_ref, pown_ref, vn_ref, v_hbm, o_ref, v_buf, sem, acc_ref):
    rows = p_ref.shape[1]
    n_new = vn_ref.shape[0]
    acc_ref[...] = jnp.zeros(acc_ref.shape, F32)

    def compute(slot, c):
        acc = acc_ref[...]
        for k in range(PAGE_CHUNK):
            acc = acc + _dot_nt(p_ref[c * PAGE_CHUNK + k], v_buf[slot, k].astype(BF16))
        acc_ref[...] = acc

    _page_stream(pt_ref, [(v_hbm, v_buf)], sem, n_chunks, compute)

    acc = acc_ref[...]
    pown = pown_ref[...]
    vn = vn_ref[...]
    for t in range(n_new):
        acc = acc + pown[:, t:t + 1] * vn[t:t + 1]
    row_head = lax.broadcasted_iota(jnp.int32, acc.shape, 0) % MOBA_HEADS
    lane_head = lax.broadcasted_iota(jnp.int32, acc.shape, 1) // MOBA_HD
    own = jnp.where(row_head == lane_head, acc, 0.0)
    o_ref[...] = jnp.sum(own.reshape(rows // MOBA_HEADS, MOBA_HEADS, MOBA_W), axis=1).astype(o_ref.dtype)


def _moba_sample(page_table, q, k_new, v_new, cache_k_t, cache_v_t):
    db, n_pages = page_table.shape
    n_chunks = n_pages // PAGE_CHUNK
    t_new = q.shape[1]
    rows = t_new * MOBA_HEADS
    pages = cache_k_t.shape[0]
    kc = cache_k_t.reshape(pages, MOBA_W, PAGE_SIZE)
    vc = cache_v_t.reshape(pages, MOBA_W, PAGE_SIZE)
    head_mask = (jnp.arange(MOBA_HEADS)[:, None] == jnp.arange(MOBA_HEADS)[None, :])
    qbd = jnp.where(head_mask[None, None, :, :, None], q[:, :, None, :, :], 0.0)
    qbd = qbd.reshape(db, rows, MOBA_W).astype(BF16)
    kn = k_new.reshape(db, t_new, MOBA_W)
    vn = v_new.reshape(db, t_new, MOBA_W)
    per_d = lambda *shape: pl.BlockSpec((None,) + shape, lambda d, pt: (d,) + (0,) * len(shape))
    buf = pltpu.VMEM((2, PAGE_CHUNK, MOBA_W, PAGE_SIZE), F32)
    p, pown = pl.pallas_call(
        functools.partial(_moba_sample_scores_kernel, n_chunks),
        grid_spec=pltpu.PrefetchScalarGridSpec(
            num_scalar_prefetch=1, grid=(db,),
            in_specs=[per_d(rows, MOBA_W), per_d(t_new, MOBA_W), pl.BlockSpec(memory_space=pl.ANY)],
            out_specs=(per_d(n_pages, rows, PAGE_SIZE), per_d(rows, LANES)),
            scratch_shapes=[buf, pltpu.SemaphoreType.DMA((2, 1)), pltpu.VMEM((n_pages, rows, PAGE_SIZE), F32)]),
        out_shape=(jax.ShapeDtypeStruct((db, n_pages, rows, PAGE_SIZE), BF16),
                   jax.ShapeDtypeStruct((db, rows, LANES), F32)),
        compiler_params=_cparams("arbitrary"), name="moba_s_scores",
    )(page_table, qbd, kn, kc)
    return pl.pallas_call(
        functools.partial(_moba_sample_pv_kernel, n_chunks),
        grid_spec=pltpu.PrefetchScalarGridSpec(
            num_scalar_prefetch=1, grid=(db,),
            in_specs=[per_d(n_pages, rows, PAGE_SIZE), per_d(rows, LANES), per_d(t_new, MOBA_W),
                      pl.BlockSpec(memory_space=pl.ANY)],
            out_specs=per_d(t_new, MOBA_W),
            scratch_shapes=[buf, pltpu.SemaphoreType.DMA((2, 1)), pltpu.VMEM((rows, MOBA_W), F32)]),
        out_shape=jax.ShapeDtypeStruct((db, t_new, MOBA_W), BF16),
        compiler_params=_cparams("arbitrary"), name="moba_s_pv",
    )(page_table, p, pown, vn, vc)


def kernel(x_prompt, x_sample, cache_mla_latent, cache_mla_krope, cache_moba_k, cache_moba_v,
           state_ffn_conv, page_table, g_attn_norm, w_in, g_qnorm, w_uq, g_kvnorm, w_uk, w_uv,
           w_br_mla, w_br_moba, w_out, g_ffn_norm, w_up, conv_w, conv_b, w_down, g_final):
    b, s, d_model = x_prompt.shape
    db, t_new, _ = x_sample.shape
    assert d_model == D_MODEL and s % ROW_TILE == 0 and (db * t_new) % ROW_TILE == 0
    assert page_table.shape == (db, PAST_LEN // PAGE_SIZE) and ROW_TILE == ATT_TILE == MOBA_BLOCK
    assert PAST_LEN % MOBA_BLOCK == 0 and ROW_TILE % t_new == 0

    pw = _prep_proj_weights(g_attn_norm, w_in, g_qnorm, w_uq, g_kvnorm, w_uk)
    tw = _prep_tail_weights(w_br_mla, w_br_moba, w_out, g_ffn_norm, w_up, conv_w, conv_b, w_down, g_final)
    wuv = _prep_wuv(w_uv)

    xp = x_prompt.reshape(b * s, D_MODEL)
    tabs_p = _rope_tables(jnp.arange(s, dtype=jnp.int32))
    ckv, kcat, ckvt, krt, kt, vt, qm, qcat, ga, gb = _proj(xp, tabs_p, s // ROW_TILE, pw)
    o_mla = _mla_prompt(qcat, kcat, ckvt, _prep_wuvt(w_uv), b, s)
    o_moba = _moba_prompt(qm, kt, vt, b, s)
    y_p, conv_p = _tail(xp, o_mla, o_moba, ga, gb, tw, s // ROW_TILE)
    c_p = ckv.reshape(b, s, KV_LORA)
    kr_p = jnp.transpose(krt, (0, 2, 1))
    k_p = jnp.transpose(kt.reshape(b, MOBA_HEADS, MOBA_HD, s), (0, 3, 1, 2))
    v_p = jnp.transpose(vt.reshape(b, MOBA_HEADS, MOBA_HD, s), (0, 3, 1, 2))

    n_s = db * t_new
    xs = x_sample.reshape(n_s, D_MODEL)
    pos_s = jnp.tile(PAST_LEN + jnp.arange(t_new, dtype=jnp.int32), db)
    tabs_s = _rope_tables(pos_s)
    ckv_s, _, _, krt_s, kt_s, vt_s, qm_s, qcat_s, ga_s, gb_s = _proj(xs, tabs_s, n_s // ROW_TILE, pw)
    c_s = ckv_s.reshape(db, t_new, KV_LORA)
    kr_s = krt_s[0].T.reshape(db, t_new, ROPE_DIM)
    k_s = kt_s[0].T.reshape(db, t_new, MOBA_HEADS, MOBA_HD)
    v_s = vt_s[0].T.reshape(db, t_new, MOBA_HEADS, MOBA_HD)
    q_rows = jnp.transpose(qcat_s[0].reshape(MLA_HEADS, db, t_new, MLA_KW), (1, 0, 2, 3))
    q_rows = q_rows.reshape(db, MLA_HEADS * t_new, MLA_KW)
    qa_rows = q_rows[..., :KV_LORA]
    qr_rows = q_rows[..., KV_LORA:KV_LORA + ROPE_DIM]
    o_lat_s = _mla_sample(page_table, qa_rows, qr_rows, c_s, kr_s, cache_mla_latent,
                          jnp.transpose(cache_mla_krope, (0, 2, 1)))
    o_lat_s = jnp.transpose(o_lat_s.reshape(db, MLA_HEADS, t_new, KV_LORA), (1, 0, 2, 3))
    o_lat_s = o_lat_s.reshape(MLA_HEADS, n_s, KV_LORA)
    o_moba_s = _moba_sample(page_table, qm_s.reshape(db, t_new, MOBA_HEADS, MOBA_HD), k_s, v_s,
                            jnp.transpose(cache_moba_k, (0, 2, 3, 1)), jnp.transpose(cache_moba_v, (0, 2, 3, 1)))
    hist = jnp.concatenate([state_ffn_conv, jnp.zeros((db, t_new - (CONV_W - 1), 2 * D_FF), F32)], axis=1)
    y_s, u_s = _tail(xs, o_lat_s, o_moba_s.reshape(n_s, MOBA_W), ga_s, gb_s, tw, 0,
                     hist=hist.reshape(n_s, -1), seq_rows=t_new, wuv=wuv, tm=SAMPLE_TILE)
    conv_s = u_s.reshape(db, t_new, 2 * D_FF)[:, t_new - (CONV_W - 1):]

    return (y_p.reshape(b, s, D_MODEL), y_s.reshape(db, t_new, D_MODEL), c_p, kr_p, k_p, v_p, conv_p,
            c_s, kr_s, k_s, v_s, conv_s)
```

```python
import functools

import jax
import jax.numpy as jnp
import numpy as np
from jax import lax
from jax.experimental import pallas as pl
from jax.experimental.pallas import tpu as pltpu

D_MODEL = 1024
PAST_LEN = 8192
PAGE_SIZE = 128

MLA_HEADS = 8
Q_LORA = 512
KV_LORA = 256
NOPE_DIM = 64
ROPE_DIM = 32
V_DIM = 64
MLA_QK = NOPE_DIM + ROPE_DIM
MLA_W = MLA_HEADS * V_DIM
MLA_SCALE = MLA_QK ** -0.5
MLA_THETA = 10000.0

MOBA_HEADS = 8
MOBA_HD = 64
MOBA_W = MOBA_HEADS * MOBA_HD
MOBA_ROT = MOBA_HD // 4
MOBA_BLOCK = 256
MOBA_TOPK = 3
MOBA_SCALE = MOBA_HD ** -0.5
ROPE_THETA = 500000.0

D_FF = 2816
CONV_W = 3
EPS = 1e-6

LANES = 128
ROW_TILE = 256
VMEM_LIMIT = 56 * 1024 * 1024
NEG_INF = float("-inf")

BF16 = jnp.bfloat16
F32 = jnp.float32


def _cparams(*sem):
    return pltpu.CompilerParams(dimension_semantics=sem, vmem_limit_bytes=VMEM_LIMIT)


def _dot(a, b):
    return jnp.dot(a, b, preferred_element_type=F32)


def _dot_nt(a, b):
    return lax.dot_general(a, b, (((1,), (1,)), ((), ())), preferred_element_type=F32)


def _rms(x, g):
    return x * lax.rsqrt(jnp.mean(x * x, axis=-1, keepdims=True) + EPS) * g


def _angles(pos, d, theta):
    half = d // 2
    inv = 1.0 / (theta ** (jnp.arange(half, dtype=F32) * (2.0 / d)))
    ang = pos.astype(F32)[:, None] * inv[None, :]
    return jnp.cos(ang), jnp.sin(ang)


def _row_table(cos, sin, group, lanes):
    half = cos.shape[1]
    p = cos.shape[0]
    lane = np.arange(lanes) % group
    first = lane < half
    second = (lane >= half) & (lane < 2 * half)
    idx = np.where(first, lane, np.where(second, lane - half, 0))
    c = jnp.where((first | second)[None, :], cos[:, idx], 1.0)
    sa = jnp.where(first[None, :], -sin[:, idx], 0.0)
    sb = jnp.where(second[None, :], sin[:, idx], 0.0)
    del p
    return jnp.concatenate([c, sa, sb], axis=1).astype(F32)


def _rope_tables(pos):
    cm, sm = _angles(pos, MOBA_ROT, ROPE_THETA)
    cr, sr = _angles(pos, ROPE_DIM, MLA_THETA)
    t_qm = _row_table(cm, sm, MOBA_HD, LANES)
    t_qr = _row_table(cr, sr, ROPE_DIM, LANES)
    t_kt = jnp.concatenate([cm.T, sm.T], axis=0)
    t_rt = jnp.concatenate([cr.T, sr.T], axis=0)
    return t_qm, t_qr, t_kt, t_rt


def _rope_rows(x, tab, half):
    c, sa, sb = tab[:, :LANES], tab[:, LANES:2 * LANES], tab[:, 2 * LANES:]
    outs = []
    for k in range(x.shape[1] // LANES):
        xk = x[:, k * LANES:(k + 1) * LANES]
        up = pltpu.roll(xk, LANES - half, axis=1)
        dn = pltpu.roll(xk, half, axis=1)
        outs.append(xk * c + up * sa + dn * sb)
    return jnp.concatenate(outs, axis=1) if len(outs) > 1 else outs[0]


W_T_ROWS = 2 * MOBA_W + ROPE_DIM
MLA_KW = KV_LORA + LANES


def _proj_kernel(x_ref, g_ref, wrow_ref, wt_ref, gq_ref, wuq_ref, gkv_ref, wuk_ref,
                 tqm_ref, tqr_ref, tkt_ref, trt_ref,
                 ckv_ref, kcat_ref, ckvt_ref, krt_ref, kt_ref, vt_ref,
                 qm_ref, qcat_ref, ga_ref, gb_ref):
    x = x_ref[...]
    h = _rms(x, g_ref[...]).astype(BF16)

    o = 0
    q_lat = _dot(h, wrow_ref[:, o:o + Q_LORA]); o += Q_LORA
    kv_lat = _dot(h, wrow_ref[:, o:o + KV_LORA]); o += KV_LORA
    q_m = _dot(h, wrow_ref[:, o:o + MOBA_W]); o += MOBA_W
    k_r = _dot(h, wrow_ref[:, o:o + LANES]); o += LANES
    ga_ref[...] = jax.nn.sigmoid(_dot(h, wrow_ref[:, o:o + D_MODEL])); o += D_MODEL
    gb_ref[...] = jax.nn.sigmoid(_dot(h, wrow_ref[:, o:o + D_MODEL]))

    ckv = _rms(kv_lat, gkv_ref[...])
    ckv_ref[...] = ckv
    kcat_ref[:, :KV_LORA] = ckv.astype(BF16)
    kcat_ref[:, KV_LORA:] = _rope_rows(k_r, tqr_ref[...], ROPE_DIM // 2).astype(BF16)
    ckvt_ref[...] = ckv.T.astype(BF16)

    qm_ref[...] = (_rope_rows(q_m, tqm_ref[...], MOBA_ROT // 2) * MOBA_SCALE).astype(BF16)

    qn = _rms(q_lat, gq_ref[...]).astype(BF16)
    q = _dot(qn, wuq_ref[...])
    q_nope = q[:, :MLA_HEADS * NOPE_DIM].astype(BF16)
    q_rope = _rope_rows(q[:, MLA_HEADS * NOPE_DIM:], tqr_ref[...], ROPE_DIM // 2)
    lane = lax.broadcasted_iota(jnp.int32, (q_rope.shape[0], LANES), 1)
    heads_per_tile = LANES // ROPE_DIM
    for j in range(MLA_HEADS // 2):
        qa2 = _dot(q_nope[:, j * LANES:(j + 1) * LANES], wuk_ref[j])
        qcat_ref[2 * j, :, :KV_LORA] = qa2[:, :KV_LORA].astype(BF16)
        qcat_ref[2 * j + 1, :, :KV_LORA] = qa2[:, KV_LORA:].astype(BF16)
    for hd in range(MLA_HEADS):
        chunk = q_rope[:, (hd // heads_per_tile) * LANES:(hd // heads_per_tile + 1) * LANES]
        shift = (hd % heads_per_tile) * ROPE_DIM
        if shift:
            chunk = pltpu.roll(chunk, LANES - shift, axis=1)
        qcat_ref[hd, :, KV_LORA:] = jnp.where(lane < ROPE_DIM, chunk, 0.0).astype(BF16)

    yt = _dot_nt(wt_ref[...], h)
    tkt = tkt_ref[...]
    cm, sm = tkt[:MOBA_ROT // 2], tkt[MOBA_ROT // 2:]
    hr = MOBA_ROT // 2
    for hd in range(MOBA_HEADS):
        r0 = hd * MOBA_HD
        x1 = yt[r0:r0 + hr]
        x2 = yt[r0 + hr:r0 + 2 * hr]
        kt_ref[r0:r0 + hr, :] = x1 * cm - x2 * sm
        kt_ref[r0 + hr:r0 + 2 * hr, :] = x2 * cm + x1 * sm
        kt_ref[r0 + 2 * hr:r0 + MOBA_HD, :] = yt[r0 + 2 * hr:r0 + MOBA_HD]
    vt_ref[...] = yt[MOBA_W:2 * MOBA_W]
    trt = trt_ref[...]
    cr, sr = trt[:ROPE_DIM // 2], trt[ROPE_DIM // 2:]
    y1 = yt[2 * MOBA_W:2 * MOBA_W + ROPE_DIM // 2]
    y2 = yt[2 * MOBA_W + ROPE_DIM // 2:]
    krt_ref[...] = jnp.concatenate([y1 * cr - y2 * sr, y2 * cr + y1 * sr], axis=0)


def _proj(x2d, tables, tiles_per_seq, wts):
    n = x2d.shape[0]
    tm = ROW_TILE
    nseq = n // (tm * tiles_per_seq)
    seq_len = tm * tiles_per_seq
    t_qm, t_qr, t_kt, t_rt = tables
    g_attn, w_row, w_t, g_q, w_uq, g_kv, w_ukp = wts

    def full(a):
        nd = a.ndim
        return pl.BlockSpec(a.shape, lambda i: (0,) * nd)

    row = lambda w: pl.BlockSpec((tm, w), lambda i: (i, 0))
    tab_row = lambda w: pl.BlockSpec((tm, w), lambda i: (i % tiles_per_seq, 0))
    tab_col = lambda r: pl.BlockSpec((r, tm), lambda i: (0, i % tiles_per_seq))
    col = lambda r: pl.BlockSpec((None, r, tm), lambda i: (i // tiles_per_seq, 0, i % tiles_per_seq))

    out_shape = (
        jax.ShapeDtypeStruct((n, KV_LORA), F32),
        jax.ShapeDtypeStruct((n, MLA_KW), BF16),
        jax.ShapeDtypeStruct((nseq, tiles_per_seq, KV_LORA, tm), BF16),
        jax.ShapeDtypeStruct((nseq, ROPE_DIM, seq_len), F32),
        jax.ShapeDtypeStruct((nseq, MOBA_W, seq_len), F32),
        jax.ShapeDtypeStruct((nseq, MOBA_W, seq_len), F32),
        jax.ShapeDtypeStruct((n, MOBA_W), BF16),
        jax.ShapeDtypeStruct((nseq, MLA_HEADS, seq_len, MLA_KW), BF16),
        jax.ShapeDtypeStruct((n, D_MODEL), F32),
        jax.ShapeDtypeStruct((n, D_MODEL), F32),
    )
    out_specs = (
        row(KV_LORA), row(MLA_KW),
        pl.BlockSpec((None, None, KV_LORA, tm), lambda i: (i // tiles_per_seq, i % tiles_per_seq, 0, 0)),
        col(ROPE_DIM), col(MOBA_W), col(MOBA_W),
        row(MOBA_W),
        pl.BlockSpec((None, MLA_HEADS, tm, MLA_KW), lambda i: (i // tiles_per_seq, 0, i % tiles_per_seq, 0)),
        row(D_MODEL), row(D_MODEL),
    )
    in_specs = [row(D_MODEL), full(g_attn), full(w_row), full(w_t), full(g_q), full(w_uq), full(g_kv),
                full(w_ukp), tab_row(3 * LANES), tab_row(3 * LANES), tab_col(MOBA_ROT), tab_col(ROPE_DIM)]
    return pl.pallas_call(
        _proj_kernel, grid=(n // tm,), in_specs=in_specs, out_specs=out_specs, out_shape=out_shape,
        compiler_params=_cparams("arbitrary"), name="proj",
    )(x2d, g_attn, w_row, w_t, g_q, w_uq, g_kv, w_ukp, t_qm, t_qr, t_kt, t_rt)


def _prep_proj_weights(g_attn_norm, w_in, g_qnorm, w_uq, g_kvnorm, w_uk):
    o_q, o_kv, o_kr = 0, Q_LORA, Q_LORA + KV_LORA
    o_qm = o_kr + ROPE_DIM
    o_km, o_vm = o_qm + MOBA_W, o_qm + 2 * MOBA_W
    o_ga = o_vm + MOBA_W
    w_kr = jnp.pad(w_in[:, o_kr:o_qm], ((0, 0), (0, LANES - ROPE_DIM)))
    w_row = jnp.concatenate([w_in[:, o_q:o_kr], w_in[:, o_qm:o_km], w_kr, w_in[:, o_ga:]], axis=1).astype(BF16)
    w_t = jnp.concatenate([w_in[:, o_km:o_vm], w_in[:, o_vm:o_ga], w_in[:, o_kr:o_qm]], axis=1).T.astype(BF16)
    wq = w_uq.reshape(Q_LORA, MLA_HEADS, MLA_QK)
    w_uq_p = jnp.concatenate([wq[:, :, :NOPE_DIM].reshape(Q_LORA, -1),
                              wq[:, :, NOPE_DIM:].reshape(Q_LORA, -1)], axis=1).astype(BF16)
    wk = jnp.transpose(w_uk, (1, 2, 0)).astype(BF16)
    z = jnp.zeros((NOPE_DIM, KV_LORA), BF16)
    w_ukp = jnp.stack([
        jnp.concatenate([jnp.concatenate([wk[2 * j], z], axis=1),
                         jnp.concatenate([z, wk[2 * j + 1]], axis=1)], axis=0)
        for j in range(MLA_HEADS // 2)])
    return (g_attn_norm.reshape(1, -1), w_row, w_t, g_qnorm.reshape(1, -1), w_uq_p,
            g_kvnorm.reshape(1, -1), w_ukp)


ATT_TILE = 256


def _mla_prompt_kernel(q_ref, kcat_ref, ckvt_ref, wuvt_ref, o_ref, m_ref, l_ref, acc_ref):
    i = pl.program_id(1)
    t = ATT_TILE
    cols = MLA_HEADS * t
    q = q_ref[...].reshape(cols, MLA_KW)
    m_ref[...] = jnp.full(m_ref.shape, NEG_INF, F32)
    l_ref[...] = jnp.zeros(l_ref.shape, F32)
    acc_ref[...] = jnp.zeros(acc_ref.shape, F32)

    def step(j, mask):
        kc = kcat_ref[pl.ds(pl.multiple_of(j * t, t), t), :]
        st = _dot_nt(kc, q) * MLA_SCALE
        if mask is not None:
            st = jnp.where(mask, st, NEG_INF)
        m_old = m_ref[...]
        m_new = jnp.maximum(m_old, jnp.max(st, axis=0, keepdims=True))
        alpha = jnp.exp(m_old - m_new)
        p = jnp.exp(st - m_new)
        l_ref[...] = alpha * l_ref[...] + jnp.sum(p, axis=0, keepdims=True)
        acc_ref[...] = alpha * acc_ref[...] + _dot(ckvt_ref[j], p.astype(BF16))
        m_ref[...] = m_new

    def body(j, carry):
        step(j, None)
        return carry

    lax.fori_loop(0, i, body, 0)
    kpos = lax.broadcasted_iota(jnp.int32, (t, cols), 0)
    qpos = lax.broadcasted_iota(jnp.int32, (t, cols), 1) % t
    step(i, kpos <= qpos)

    o_lat = (acc_ref[...] / l_ref[...]).astype(BF16)
    o_t = jnp.concatenate([_dot(wuvt_ref[h], o_lat[:, h * t:(h + 1) * t]) for h in range(MLA_HEADS)], axis=0)
    o_ref[...] = o_t.T.astype(BF16)


def _prep_wuv(w_uv):
    w = jnp.transpose(w_uv, (1, 0, 2)).astype(BF16)
    z = jnp.zeros_like(w)
    even = jnp.concatenate([w, z], axis=2)
    odd = jnp.concatenate([z, w], axis=2)
    is_even = (jnp.arange(MLA_HEADS) % 2 == 0)[:, None, None]
    return jnp.where(is_even, even, odd)


def _prep_wuvt(w_uv):
    return jnp.transpose(w_uv, (1, 2, 0)).astype(BF16)


def _mla_prompt(qcat, kcat, ckvt, wuvt, b, s):
    t = ATT_TILE
    nq = s // t
    cols = MLA_HEADS * t
    return pl.pallas_call(
        _mla_prompt_kernel, grid=(b, nq),
        in_specs=[
            pl.BlockSpec((None, MLA_HEADS, t, MLA_KW), lambda bi, i: (bi, 0, i, 0)),
            pl.BlockSpec((s, MLA_KW), lambda bi, i: (bi, 0)),
            pl.BlockSpec((None, nq, KV_LORA, t), lambda bi, i: (bi, 0, 0, 0)),
            pl.BlockSpec(wuvt.shape, lambda bi, i: (0, 0, 0)),
        ],
        out_specs=pl.BlockSpec((t, MLA_W), lambda bi, i: (bi * nq + i, 0)),
        out_shape=jax.ShapeDtypeStruct((b * s, MLA_W), BF16),
        scratch_shapes=[pltpu.VMEM((1, cols), F32), pltpu.VMEM((1, cols), F32), pltpu.VMEM((KV_LORA, cols), F32)],
        compiler_params=_cparams("arbitrary", "arbitrary"), name="mla_p",
    )(qcat, kcat, ckvt, wuvt)


def _moba_attend(nb, q, krow_ref, vbf_ref, o_ref):
    t = MOBA_BLOCK
    nk = nb * t
    lane = lax.broadcasted_iota(jnp.int32, (t, LANES), 1)
    causal = lax.broadcasted_iota(jnp.int32, (t, t), 0) <= lax.broadcasted_iota(jnp.int32, (t, t), 1)
    n_past = nb - 1
    halves = []
    for half in range(2):
        qh = jnp.where((lane >= MOBA_HD) == bool(half), q, jnp.zeros_like(q))
        st = _dot_nt(krow_ref[0:nk, :], qh)
        blks = [st[n * t:(n + 1) * t] for n in range(nb)]
        blks[-1] = jnp.where(causal, blks[-1], NEG_INF)
        bmax = [jnp.max(b, axis=0, keepdims=True) for b in blks]
        mx = bmax[-1]
        if n_past > MOBA_TOPK:
            bsum = [jnp.sum(b, axis=0, keepdims=True) for b in blks[:-1]]
            keep = []
            for n in range(n_past):
                rank = jnp.zeros((1, t), F32)
                for m in range(n_past):
                    if m != n:
                        beats = (bsum[m] >= bsum[n]) if m < n else (bsum[m] > bsum[n])
                        rank = rank + beats.astype(F32)
                keep.append(rank < MOBA_TOPK)
                mx = jnp.maximum(mx, jnp.where(keep[n], bmax[n], NEG_INF))
        else:
            keep = [None] * n_past
            for n in range(n_past):
                mx = jnp.maximum(mx, bmax[n])
        l = jnp.zeros((1, t), F32)
        ps = []
        for n in range(nb):
            p = jnp.exp(blks[n] - mx)
            if n < n_past and keep[n] is not None:
                p = jnp.where(keep[n], p, 0.0)
            l = l + jnp.sum(p, axis=0, keepdims=True)
            ps.append(p.astype(BF16))
        pt = jnp.concatenate(ps, axis=0) if nb > 1 else ps[0]
        halves.append(_dot(vbf_ref[:, 0:nk], pt) / l)
    sub = lax.broadcasted_iota(jnp.int32, (LANES, t), 0)
    o_ref[...] = jnp.where(sub < MOBA_HD, halves[0], halves[1]).T.astype(BF16)


def _moba_prompt_kernel(nblk, q_ref, kt_ref, vt_ref, o_ref, krow_ref, vbf_ref):
    i = pl.program_id(2)

    @pl.when(i == 0)
    def _():
        krow_ref[...] = kt_ref[...].T.astype(BF16)
        vbf_ref[...] = vt_ref[...].astype(BF16)

    q = q_ref[...]
    for nb in range(1, nblk + 1):
        pl.when(i == nb - 1)(functools.partial(_moba_attend, nb, q, krow_ref, vbf_ref, o_ref))


def _moba_prompt(qm, kt, vt, b, s):
    t = MOBA_BLOCK
    nq = s // t
    npair = MOBA_W // LANES
    return pl.pallas_call(
        functools.partial(_moba_prompt_kernel, nq), grid=(b, npair, nq),
        in_specs=[
            pl.BlockSpec((t, LANES), lambda bi, j, i: (bi * nq + i, j)),
            pl.BlockSpec((None, LANES, s), lambda bi, j, i: (bi, j, 0)),
            pl.BlockSpec((None, LANES, s), lambda bi, j, i: (bi, j, 0)),
        ],
        out_specs=pl.BlockSpec((t, LANES), lambda bi, j, i: (bi * nq + i, j)),
        out_shape=jax.ShapeDtypeStruct((b * s, MOBA_W), BF16),
        scratch_shapes=[pltpu.VMEM((s, LANES), BF16), pltpu.VMEM((LANES, s), BF16)],
        compiler_params=_cparams("arbitrary", "arbitrary", "arbitrary"), name="moba_p",
    )(qm, kt, vt)


FF_CHUNK = D_FF // 2
HIST_ROWS = 8
SAMPLE_TILE = 128


def _tail_kernel(seq_tiles, seq_rows, *refs):
    per_row_hist = seq_tiles == 0
    if per_row_hist:
        (x_ref, olat_ref, ob_ref, ga_ref, gb_ref, hist_ref, wuv_ref, wa_ref, wb_ref, wo_ref, gf_ref,
         wup_ref, cw_ref, cb_ref, wdn_ref, gfin_ref, y_ref, u_ref, ue_ref) = refs
        om = jnp.concatenate(
            [_dot(olat_ref[2 * j], wuv_ref[2 * j]) + _dot(olat_ref[2 * j + 1], wuv_ref[2 * j + 1])
             for j in range(MLA_HEADS // 2)], axis=1).astype(BF16)
    else:
        (x_ref, om_ref, ob_ref, ga_ref, gb_ref, wa_ref, wb_ref, wo_ref, gf_ref, wup_ref,
         cw_ref, cb_ref, wdn_ref, gfin_ref, y_ref, conv_ref, ue_ref, carry_ref) = refs
        om = om_ref[...]
    i = pl.program_id(0)
    tm = x_ref.shape[0]
    cw = FF_CHUNK

    a = _dot(om, wa_ref[...])
    b = _dot(ob_ref[...], wb_ref[...])
    mg = (ga_ref[...] * a + gb_ref[...] * b).astype(BF16)
    x1 = x_ref[...] + _dot(mg, wo_ref[...])
    hn = _rms(x1, gf_ref[...]).astype(BF16)

    if per_row_hist:
        tpos = lax.broadcasted_iota(jnp.int32, (tm, cw), 0) % seq_rows
        ue_ref[0:HIST_ROWS, :] = jnp.zeros((HIST_ROWS, cw), F32)
    else:
        first = (i % seq_tiles) == 0

        @pl.when(i == 0)
        def _():
            carry_ref[...] = jnp.zeros(carry_ref.shape, F32)

    acc = x1
    for c in range(D_FF // cw):
        halves = []
        for half in range(2):
            c0 = half * D_FF + c * cw
            u = _dot(hn, wup_ref[:, c0:c0 + cw])
            if not per_row_hist:
                ue_ref[0:HIST_ROWS, :] = jnp.where(first, 0.0, carry_ref[:, c0:c0 + cw])
            ue_ref[HIST_ROWS:, :] = u
            u1 = ue_ref[HIST_ROWS - 1:HIST_ROWS - 1 + tm, :]
            u2 = ue_ref[HIST_ROWS - 2:HIST_ROWS - 2 + tm, :]
            if per_row_hist:
                hh = hist_ref[:, c0:c0 + cw]
                u1 = jnp.where(tpos < 1, pltpu.roll(hh, tm - 1, axis=0), u1)
                u2 = jnp.where(tpos < 2, hh, u2)
                u_ref[:, c0:c0 + cw] = u
            else:
                carry_ref[:, c0:c0 + cw] = u[tm - HIST_ROWS:, :]
                conv_ref[:, c0:c0 + cw] = u[tm - (CONV_W - 1):, :]
            halves.append(cb_ref[:, c0:c0 + cw] + cw_ref[0:1, c0:c0 + cw] * u2
                          + cw_ref[1:2, c0:c0 + cw] * u1 + cw_ref[2:3, c0:c0 + cw] * u)
        act = (jax.nn.silu(halves[0]) * halves[1]).astype(BF16)
        acc = acc + _dot(act, wdn_ref[c * cw:(c + 1) * cw, :])
    y_ref[...] = _rms(acc, gfin_ref[...])


def _tail(x2d, o_mla, o_moba, ga, gb, wts, seq_tiles, hist=None, seq_rows=0, wuv=None, tm=ROW_TILE):
    n = x2d.shape[0]
    row = lambda w: pl.BlockSpec((tm, w), lambda i: (i, 0))

    def const(a):
        nd = a.ndim
        return pl.BlockSpec(a.shape, lambda i: (0,) * nd, pipeline_mode=pl.Buffered(1))

    acts = [x2d, o_mla, o_moba, ga, gb]
    act_specs = [row(D_MODEL), row(MLA_W), row(MOBA_W), row(D_MODEL), row(D_MODEL)]
    scratch = [pltpu.VMEM((tm + HIST_ROWS, FF_CHUNK), F32)]
    if seq_tiles == 0:
        acts.append(hist)
        act_specs[1] = pl.BlockSpec((MLA_HEADS, tm, KV_LORA), lambda i: (0, i, 0))
        act_specs.append(row(2 * D_FF))
        wts = (wuv,) + tuple(wts)
        out_shape = (jax.ShapeDtypeStruct((n, D_MODEL), F32), jax.ShapeDtypeStruct((n, 2 * D_FF), F32))
        out_specs = (row(D_MODEL), row(2 * D_FF))
    else:
        nseq = n // (tm * seq_tiles)
        out_shape = (jax.ShapeDtypeStruct((n, D_MODEL), F32),
                     jax.ShapeDtypeStruct((nseq, CONV_W - 1, 2 * D_FF), F32))
        out_specs = (row(D_MODEL), pl.BlockSpec((None, CONV_W - 1, 2 * D_FF), lambda i: (i // seq_tiles, 0, 0)))
        scratch.append(pltpu.VMEM((HIST_ROWS, 2 * D_FF), F32))
    return pl.pallas_call(
        functools.partial(_tail_kernel, seq_tiles, seq_rows), grid=(n // tm,),
        in_specs=act_specs + [const(w) for w in wts], out_specs=out_specs, out_shape=out_shape,
        scratch_shapes=scratch, compiler_params=_cparams("arbitrary"), name="tail",
    )(*acts, *wts)


def _prep_tail_weights(w_br_mla, w_br_moba, w_out, g_ffn_norm, w_up, conv_w, conv_b, w_down, g_final):
    return (w_br_mla.astype(BF16), w_br_moba.astype(BF16), w_out.astype(BF16), g_ffn_norm.reshape(1, -1),
            w_up.astype(BF16), conv_w, conv_b.reshape(1, -1), w_down.astype(BF16), g_final.reshape(1, -1))


PAGE_CHUNK = 32


def _page_stream(pt_ref, hbm_bufs, sem, n_chunks, compute):
    d = pl.program_id(0)
    total = pl.num_programs(0) * n_chunks

    def copies(g, slot):
        dd = g // n_chunks
        c = g % n_chunks
        out = []
        for k in range(PAGE_CHUNK):
            page = pt_ref[dd, c * PAGE_CHUNK + k]
            for idx, (hbm, buf) in enumerate(hbm_bufs):
                out.append(pltpu.make_async_copy(hbm.at[page], buf.at[slot, k], sem.at[slot, idx]))
        return out

    @pl.when(d == 0)
    def _():
        for cp in copies(0, 0):
            cp.start()

    def body(c, carry):
        g = d * n_chunks + c
        slot = g % 2

        @pl.when(g + 1 < total)
        def _():
            for cp in copies(g + 1, 1 - slot):
                cp.start()

        for cp in copies(g, slot):
            cp.wait()
        compute(slot, c)
        return carry

    lax.fori_loop(0, n_chunks, body, 0)


def _mla_sample_kernel(n_chunks, pt_ref, qa_ref, qr_ref, cn_ref, krn_ref, lat_hbm, kr_hbm, o_ref,
                       lat_buf, kr_buf, sem, m_ref, l_ref, acc_ref):
    qa = qa_ref[...]
    qr = qr_ref[...]
    rows = qa.shape[0]
    n_new = cn_ref.shape[0]
    m_ref[...] = jnp.full(m_ref.shape, NEG_INF, F32)
    l_ref[...] = jnp.zeros(l_ref.shape, F32)
    acc_ref[...] = jnp.zeros(acc_ref.shape, F32)

    def compute(slot, c):
        del c
        cb = lat_buf[slot].reshape(PAGE_CHUNK * PAGE_SIZE, KV_LORA).astype(BF16)
        kr = jnp.concatenate([kr_buf[slot, k] for k in range(PAGE_CHUNK)], axis=1).astype(BF16)
        s = (_dot_nt(qa, cb) + _dot(qr, kr)) * MLA_SCALE
        m_old = m_ref[...]
        m_new = jnp.maximum(m_old, jnp.max(s, axis=1, keepdims=True))
        alpha = jnp.exp(m_old - m_new)
        p = jnp.exp(s - m_new)
        l_ref[...] = alpha * l_ref[...] + jnp.sum(p, axis=1, keepdims=True)
        acc_ref[...] = alpha * acc_ref[...] + _dot(p.astype(BF16), cb)
        m_ref[...] = m_new

    _page_stream(pt_ref, [(lat_hbm, lat_buf), (kr_hbm, kr_buf)], sem, n_chunks, compute)

    qaf = qa.astype(F32)
    qrf = qr.astype(F32)
    cn = cn_ref[...]
    krn = krn_ref[...]
    tq = lax.broadcasted_iota(jnp.int32, (rows, 1), 0) % n_new
    s_new = []
    for t in range(n_new):
        st = (jnp.sum(qaf * cn[t:t + 1], axis=1, keepdims=True)
              + jnp.sum(qrf * krn[t:t + 1], axis=1, keepdims=True)) * MLA_SCALE
        s_new.append(jnp.where(t <= tq, st, NEG_INF))
    m_old = m_ref[...]
    m_new = functools.reduce(jnp.maximum, s_new, m_old)
    alpha = jnp.exp(m_old - m_new)
    l = alpha * l_ref[...]
    acc = alpha * acc_ref[...]
    for t in range(n_new):
        p = jnp.exp(s_new[t] - m_new)
        l = l + p
        acc = acc + p * cn[t:t + 1]
    o_ref[...] = (acc / l).astype(o_ref.dtype)


def _mla_sample(page_table, qa_rows, qr_rows, c_new, kr_new, cache_lat, cache_kr_t):
    db, n_pages = page_table.shape
    n_chunks = n_pages // PAGE_CHUNK
    rows = qa_rows.shape[1]
    t_new = c_new.shape[1]
    per_d = lambda *shape: pl.BlockSpec((None,) + shape, lambda d, pt: (d,) + (0,) * len(shape))
    grid_spec = pltpu.PrefetchScalarGridSpec(
        num_scalar_prefetch=1, grid=(db,),
        in_specs=[per_d(rows, KV_LORA), per_d(rows, ROPE_DIM), per_d(t_new, KV_LORA), per_d(t_new, ROPE_DIM),
                  pl.BlockSpec(memory_space=pl.ANY), pl.BlockSpec(memory_space=pl.ANY)],
        out_specs=per_d(rows, KV_LORA),
        scratch_shapes=[pltpu.VMEM((2, PAGE_CHUNK, PAGE_SIZE, KV_LORA), F32),
                        pltpu.VMEM((2, PAGE_CHUNK, ROPE_DIM, PAGE_SIZE), F32),
                        pltpu.SemaphoreType.DMA((2, 2)),
                        pltpu.VMEM((rows, 1), F32), pltpu.VMEM((rows, 1), F32), pltpu.VMEM((rows, KV_LORA), F32)])
    return pl.pallas_call(
        functools.partial(_mla_sample_kernel, n_chunks), grid_spec=grid_spec,
        out_shape=jax.ShapeDtypeStruct((db, rows, KV_LORA), BF16),
        compiler_params=_cparams("arbitrary"), name="mla_s",
    )(page_table, qa_rows, qr_rows, c_new, kr_new, cache_lat, cache_kr_t)


PAGES_PER_BLOCK = MOBA_BLOCK // PAGE_SIZE


def _moba_sample_scores_kernel(n_chunks, pt_ref, q_ref, kn_ref, k_hbm, psel_ref, idx_ref, pown_ref,
                               k_buf, sem, s_ref):
    q = q_ref[...]
    rows = q.shape[0]
    n_new = kn_ref.shape[0]
    n_pages = n_chunks * PAGE_CHUNK
    n_blocks = n_pages // PAGES_PER_BLOCK

    def compute(slot, c):
        for k in range(PAGE_CHUNK):
            s_ref[c * PAGE_CHUNK + k] = _dot(q, k_buf[slot, k].astype(BF16))

    _page_stream(pt_ref, [(k_hbm, k_buf)], sem, n_chunks, compute)

    lane = lax.broadcasted_iota(jnp.int32, (rows, LANES), 1)
    blk = lambda n: jnp.concatenate([s_ref[PAGES_PER_BLOCK * n + k] for k in range(PAGES_PER_BLOCK)], axis=1)
    bsum = [jnp.sum(blk(n), axis=1, keepdims=True) for n in range(n_blocks)]
    bs = jnp.full((rows, LANES), NEG_INF, F32)
    for n in range(n_blocks):
        bs = jnp.where(lane == n, bsum[n], bs)
    rank = []
    for n in range(n_blocks):
        beats = jnp.logical_or(bs > bsum[n], jnp.logical_and(bs == bsum[n], lane < n))
        rank.append(jnp.sum(beats.astype(F32), axis=1, keepdims=True))
    s_sel = [jnp.zeros((rows, MOBA_BLOCK), F32) for _ in range(MOBA_TOPK)]
    idx = jnp.zeros((rows, LANES), F32)
    for n in range(n_blocks):
        b = blk(n)
        for r in range(MOBA_TOPK):
            hit = rank[n] == float(r)
            s_sel[r] = jnp.where(hit, b, s_sel[r])
            idx = jnp.where(jnp.logical_and(hit, lane == r), float(n), idx)

    qf = q.astype(F32)
    kn = kn_ref[...]
    tq = lax.broadcasted_iota(jnp.int32, (rows, 1), 0) % n_new
    s_new = [jnp.where(t <= tq, jnp.sum(qf * kn[t:t + 1], axis=1, keepdims=True), NEG_INF) for t in range(n_new)]

    m = functools.reduce(jnp.maximum, s_new + [jnp.max(s, axis=1, keepdims=True) for s in s_sel])
    p_new = [jnp.exp(s - m) for s in s_new]
    p_sel = [jnp.exp(s - m) for s in s_sel]
    l = functools.reduce(lambda a, b: a + b, p_new + [jnp.sum(p, axis=1, keepdims=True) for p in p_sel])
    inv = 1.0 / l
    for r in range(MOBA_TOPK):
        psel_ref[:, r * MOBA_BLOCK:(r + 1) * MOBA_BLOCK] = p_sel[r] * inv
    idx_ref[...] = idx
    pown = jnp.zeros((rows, LANES), F32)
    for t in range(n_new):
        pown = jnp.where(lane == t, p_new[t] * inv, pown)
    pown_ref[...] = pown


SEL_KEYS = MOBA_TOPK * MOBA_BLOCK


def _moba_sample_pv_kernel(n_new, pt_ref, sel_ref, psel_ref, pown_ref, vn_ref, v_hbm, o_ref, v_buf, sem):
    d = pl.program_id(0)
    nd = pl.num_programs(0)
    rows = MOBA_HEADS * n_new

    def copies(dd, slot):
        out = []
        for h in range(MOBA_HEADS):
            for t in range(n_new):
                for r in range(MOBA_TOPK):
                    blk = sel_ref[dd, (h * n_new + t) * MOBA_TOPK + r]
                    for k in range(PAGES_PER_BLOCK):
                        page = pt_ref[dd, blk * PAGES_PER_BLOCK + k]
                        off = ((t * MOBA_TOPK + r) * PAGES_PER_BLOCK + k) * PAGE_SIZE
                        out.append(pltpu.make_async_copy(
                            v_hbm.at[page, h], v_buf.at[slot, h, :, pl.ds(off, PAGE_SIZE)], sem.at[slot]))
        return out

    @pl.when(d == 0)
    def _():
        for cp in copies(0, 0):
            cp.start()

    slot = d % 2

    @pl.when(d + 1 < nd)
    def _():
        for cp in copies(d + 1, 1 - slot):
            cp.start()

    for cp in copies(d, slot):
        cp.wait()

    width = n_new * SEL_KEYS
    p = psel_ref[...]
    seg = lax.broadcasted_iota(jnp.int32, (rows, width), 1) // SEL_KEYS
    tok = lax.broadcasted_iota(jnp.int32, (rows, width), 0) % n_new
    p_all = jnp.where(seg == tok, jnp.concatenate([p] * n_new, axis=1), 0.0).astype(BF16)
    pown = pown_ref[...]
    row_head = lax.broadcasted_iota(jnp.int32, (rows, MOBA_HD), 0) // n_new
    acc = jnp.zeros((rows, MOBA_HD), F32)
    for h in range(MOBA_HEADS):
        o_h = _dot_nt(p_all, v_buf[slot, h].astype(BF16))
        vn = vn_ref[h]
        for t in range(n_new):
            o_h = o_h + pown[:, t:t + 1] * vn[t:t + 1]
        acc = jnp.where(row_head == h, o_h, acc)
    o_ref[...] = acc.astype(o_ref.dtype)


def _moba_sample(page_table, q, k_new, v_new, cache_k_t, cache_v_t):
    db, n_pages = page_table.shape
    n_chunks = n_pages // PAGE_CHUNK
    t_new = q.shape[1]
    rows = t_new * MOBA_HEADS
    pages = cache_k_t.shape[0]
    assert n_pages // PAGES_PER_BLOCK >= MOBA_TOPK
    kc = cache_k_t.reshape(pages, MOBA_W, PAGE_SIZE)
    head_mask = (jnp.arange(MOBA_HEADS)[:, None] == jnp.arange(MOBA_HEADS)[None, :])
    qht = jnp.transpose(q, (0, 2, 1, 3))
    qbd = jnp.where(head_mask[None, :, None, :, None], qht[:, :, :, None, :], 0.0)
    qbd = qbd.reshape(db, rows, MOBA_W).astype(BF16)
    kn = k_new.reshape(db, t_new, MOBA_W)
    vn = jnp.transpose(v_new, (0, 2, 1, 3))
    per_d = lambda *shape: pl.BlockSpec((None,) + shape, lambda d, *_: (d,) + (0,) * len(shape))
    psel, idx, pown = pl.pallas_call(
        functools.partial(_moba_sample_scores_kernel, n_chunks),
        grid_spec=pltpu.PrefetchScalarGridSpec(
            num_scalar_prefetch=1, grid=(db,),
            in_specs=[per_d(rows, MOBA_W), per_d(t_new, MOBA_W), pl.BlockSpec(memory_space=pl.ANY)],
            out_specs=(per_d(rows, SEL_KEYS), per_d(rows, LANES), per_d(rows, LANES)),
            scratch_shapes=[pltpu.VMEM((2, PAGE_CHUNK, MOBA_W, PAGE_SIZE), F32), pltpu.SemaphoreType.DMA((2, 1)),
                            pltpu.VMEM((n_pages, rows, PAGE_SIZE), F32)]),
        out_shape=(jax.ShapeDtypeStruct((db, rows, SEL_KEYS), F32),
                   jax.ShapeDtypeStruct((db, rows, LANES), F32),
                   jax.ShapeDtypeStruct((db, rows, LANES), F32)),
        compiler_params=_cparams("arbitrary"), name="moba_s_scores",
    )(page_table, qbd, kn, kc)
    sel = idx[:, :, :MOBA_TOPK].astype(jnp.int32).reshape(db, rows * MOBA_TOPK)
    o = pl.pallas_call(
        functools.partial(_moba_sample_pv_kernel, t_new),
        grid_spec=pltpu.PrefetchScalarGridSpec(
            num_scalar_prefetch=2, grid=(db,),
            in_specs=[per_d(rows, SEL_KEYS), per_d(rows, LANES), per_d(MOBA_HEADS, t_new, MOBA_HD),
                      pl.BlockSpec(memory_space=pl.ANY)],
            out_specs=per_d(rows, MOBA_HD),
            scratch_shapes=[pltpu.VMEM((2, MOBA_HEADS, MOBA_HD, t_new * SEL_KEYS), F32),
                            pltpu.SemaphoreType.DMA((2,))]),
        out_shape=jax.ShapeDtypeStruct((db, rows, MOBA_HD), BF16),
        compiler_params=_cparams("arbitrary"), name="moba_s_pv",
    )(page_table, sel, psel, pown, vn, cache_v_t)
    return jnp.transpose(o.reshape(db, MOBA_HEADS, t_new, MOBA_HD), (0, 2, 1, 3)).reshape(db, t_new, MOBA_W)


def kernel(x_prompt, x_sample, cache_mla_latent, cache_mla_krope, cache_moba_k, cache_moba_v,
           state_ffn_conv, page_table, g_attn_norm, w_in, g_qnorm, w_uq, g_kvnorm, w_uk, w_uv,
           w_br_mla, w_br_moba, w_out, g_ffn_norm, w_up, conv_w, conv_b, w_down, g_final):
    b, s, d_model = x_prompt.shape
    db, t_new, _ = x_sample.shape
    assert d_model == D_MODEL and s % ROW_TILE == 0 and (db * t_new) % ROW_TILE == 0
    assert page_table.shape == (db, PAST_LEN // PAGE_SIZE) and ROW_TILE == ATT_TILE == MOBA_BLOCK
    assert PAST_LEN % MOBA_BLOCK == 0 and ROW_TILE % t_new == 0

    pw = _prep_proj_weights(g_attn_norm, w_in, g_qnorm, w_uq, g_kvnorm, w_uk)
    tw = _prep_tail_weights(w_br_mla, w_br_moba, w_out, g_ffn_norm, w_up, conv_w, conv_b, w_down, g_final)
    wuv = _prep_wuv(w_uv)

    xp = x_prompt.reshape(b * s, D_MODEL)
    tabs_p = _rope_tables(jnp.arange(s, dtype=jnp.int32))
    ckv, kcat, ckvt, krt, kt, vt, qm, qcat, ga, gb = _proj(xp, tabs_p, s // ROW_TILE, pw)
    o_mla = _mla_prompt(qcat, kcat, ckvt, _prep_wuvt(w_uv), b, s)
    o_moba = _moba_prompt(qm, kt, vt, b, s)
    y_p, conv_p = _tail(xp, o_mla, o_moba, ga, gb, tw, s // ROW_TILE)
    c_p = ckv.reshape(b, s, KV_LORA)
    kr_p = jnp.transpose(krt, (0, 2, 1))
    k_p = jnp.transpose(kt.reshape(b, MOBA_HEADS, MOBA_HD, s), (0, 3, 1, 2))
    v_p = jnp.transpose(vt.reshape(b, MOBA_HEADS, MOBA_HD, s), (0, 3, 1, 2))

    n_s = db * t_new
    xs = x_sample.reshape(n_s, D_MODEL)
    pos_s = jnp.tile(PAST_LEN + jnp.arange(t_new, dtype=jnp.int32), db)
    tabs_s = _rope_tables(pos_s)
    ckv_s, _, _, krt_s, kt_s, vt_s, qm_s, qcat_s, ga_s, gb_s = _proj(xs, tabs_s, n_s // ROW_TILE, pw)
    c_s = ckv_s.reshape(db, t_new, KV_LORA)
    kr_s = krt_s[0].T.reshape(db, t_new, ROPE_DIM)
    k_s = kt_s[0].T.reshape(db, t_new, MOBA_HEADS, MOBA_HD)
    v_s = vt_s[0].T.reshape(db, t_new, MOBA_HEADS, MOBA_HD)
    q_rows = jnp.transpose(qcat_s[0].reshape(MLA_HEADS, db, t_new, MLA_KW), (1, 0, 2, 3))
    q_rows = q_rows.reshape(db, MLA_HEADS * t_new, MLA_KW)
    qa_rows = q_rows[..., :KV_LORA]
    qr_rows = q_rows[..., KV_LORA:KV_LORA + ROPE_DIM]
    o_lat_s = _mla_sample(page_table, qa_rows, qr_rows, c_s, kr_s, cache_mla_latent,
                          jnp.transpose(cache_mla_krope, (0, 2, 1)))
    o_lat_s = jnp.transpose(o_lat_s.reshape(db, MLA_HEADS, t_new, KV_LORA), (1, 0, 2, 3))
    o_lat_s = o_lat_s.reshape(MLA_HEADS, n_s, KV_LORA)
    o_moba_s = _moba_sample(page_table, qm_s.reshape(db, t_new, MOBA_HEADS, MOBA_HD), k_s, v_s,
                            jnp.transpose(cache_moba_k, (0, 2, 3, 1)), jnp.transpose(cache_moba_v, (0, 2, 3, 1)))
    hist = jnp.concatenate([state_ffn_conv, jnp.zeros((db, t_new - (CONV_W - 1), 2 * D_FF), F32)], axis=1)
    y_s, u_s = _tail(xs, o_lat_s, o_moba_s.reshape(n_s, MOBA_W), ga_s, gb_s, tw, 0,
                     hist=hist.reshape(n_s, -1), seq_rows=t_new, wuv=wuv, tm=SAMPLE_TILE)
    conv_s = u_s.reshape(db, t_new, 2 * D_FF)[:, t_new - (CONV_W - 1):]

    return (y_p.reshape(b, s, D_MODEL), y_s.reshape(db, t_new, D_MODEL), c_p, kr_p, k_p, v_p, conv_p,
            c_s, kr_s, k_s, v_s, conv_s)
```

```python
import functools

import jax
import jax.numpy as jnp
import numpy as np
from jax import lax
from jax.experimental import pallas as pl
from jax.experimental.pallas import tpu as pltpu

D_MODEL = 1024
PAST_LEN = 8192
PAGE_SIZE = 128

MLA_HEADS = 8
Q_LORA = 512
KV_LORA = 256
NOPE_DIM = 64
ROPE_DIM = 32
V_DIM = 64
MLA_QK = NOPE_DIM + ROPE_DIM
MLA_W = MLA_HEADS * V_DIM
MLA_SCALE = MLA_QK ** -0.5
MLA_THETA = 10000.0

MOBA_HEADS = 8
MOBA_HD = 64
MOBA_W = MOBA_HEADS * MOBA_HD
MOBA_ROT = MOBA_HD // 4
MOBA_BLOCK = 256
MOBA_TOPK = 3
MOBA_SCALE = MOBA_HD ** -0.5
ROPE_THETA = 500000.0

D_FF = 2816
CONV_W = 3
EPS = 1e-6

LANES = 128
ROW_TILE = 256
VMEM_LIMIT = 56 * 1024 * 1024
NEG_INF = float("-inf")

BF16 = jnp.bfloat16
F32 = jnp.float32


def _cparams(*sem):
    return pltpu.CompilerParams(dimension_semantics=sem, vmem_limit_bytes=VMEM_LIMIT)


def _dot(a, b):
    return jnp.dot(a, b, preferred_element_type=F32)


def _dot_nt(a, b):
    return lax.dot_general(a, b, (((1,), (1,)), ((), ())), preferred_element_type=F32)


def _rms(x, g):
    return x * lax.rsqrt(jnp.mean(x * x, axis=-1, keepdims=True) + EPS) * g


def _angles(pos, d, theta):
    half = d // 2
    inv = 1.0 / (theta ** (jnp.arange(half, dtype=F32) * (2.0 / d)))
    ang = pos.astype(F32)[:, None] * inv[None, :]
    return jnp.cos(ang), jnp.sin(ang)


def _row_table(cos, sin, group, lanes):
    half = cos.shape[1]
    p = cos.shape[0]
    lane = np.arange(lanes) % group
    first = lane < half
    second = (lane >= half) & (lane < 2 * half)
    idx = np.where(first, lane, np.where(second, lane - half, 0))
    c = jnp.where((first | second)[None, :], cos[:, idx], 1.0)
    sa = jnp.where(first[None, :], -sin[:, idx], 0.0)
    sb = jnp.where(second[None, :], sin[:, idx], 0.0)
    del p
    return jnp.concatenate([c, sa, sb], axis=1).astype(F32)


def _rope_tables(pos):
    cm, sm = _angles(pos, MOBA_ROT, ROPE_THETA)
    cr, sr = _angles(pos, ROPE_DIM, MLA_THETA)
    t_qm = _row_table(cm, sm, MOBA_HD, LANES)
    t_qr = _row_table(cr, sr, ROPE_DIM, LANES)
    t_kt = jnp.concatenate([cm.T, sm.T], axis=0)
    t_rt = jnp.concatenate([cr.T, sr.T], axis=0)
    return t_qm, t_qr, t_kt, t_rt


def _rope_rows(x, tab, half):
    c, sa, sb = tab[:, :LANES], tab[:, LANES:2 * LANES], tab[:, 2 * LANES:]
    outs = []
    for k in range(x.shape[1] // LANES):
        xk = x[:, k * LANES:(k + 1) * LANES]
        up = pltpu.roll(xk, LANES - half, axis=1)
        dn = pltpu.roll(xk, half, axis=1)
        outs.append(xk * c + up * sa + dn * sb)
    return jnp.concatenate(outs, axis=1) if len(outs) > 1 else outs[0]


W_T_ROWS = 2 * MOBA_W + ROPE_DIM
MLA_KW = KV_LORA + LANES


def _proj_kernel(x_ref, g_ref, wrow_ref, wt_ref, gq_ref, wuq_ref, gkv_ref, wuk_ref,
                 tqm_ref, tqr_ref, tkt_ref, trt_ref,
                 ckv_ref, kcat_ref, ckvt_ref, krt_ref, kt_ref, vt_ref,
                 qm_ref, qcat_ref, ga_ref, gb_ref):
    x = x_ref[...]
    h = _rms(x, g_ref[...]).astype(BF16)

    o = 0
    q_lat = _dot(h, wrow_ref[:, o:o + Q_LORA]); o += Q_LORA
    kv_lat = _dot(h, wrow_ref[:, o:o + KV_LORA]); o += KV_LORA
    q_m = _dot(h, wrow_ref[:, o:o + MOBA_W]); o += MOBA_W
    k_r = _dot(h, wrow_ref[:, o:o + LANES]); o += LANES
    ga_ref[...] = jax.nn.sigmoid(_dot(h, wrow_ref[:, o:o + D_MODEL])); o += D_MODEL
    gb_ref[...] = jax.nn.sigmoid(_dot(h, wrow_ref[:, o:o + D_MODEL]))

    ckv = _rms(kv_lat, gkv_ref[...])
    ckv_ref[...] = ckv
    kcat_ref[:, :KV_LORA] = ckv.astype(BF16)
    kcat_ref[:, KV_LORA:] = _rope_rows(k_r, tqr_ref[...], ROPE_DIM // 2).astype(BF16)
    ckvt_ref[...] = ckv.T.astype(BF16)

    qm_ref[...] = (_rope_rows(q_m, tqm_ref[...], MOBA_ROT // 2) * MOBA_SCALE).astype(BF16)

    qn = _rms(q_lat, gq_ref[...]).astype(BF16)
    q = _dot(qn, wuq_ref[...])
    q_nope = q[:, :MLA_HEADS * NOPE_DIM].astype(BF16)
    q_rope = _rope_rows(q[:, MLA_HEADS * NOPE_DIM:], tqr_ref[...], ROPE_DIM // 2)
    lane = lax.broadcasted_iota(jnp.int32, (q_rope.shape[0], LANES), 1)
    heads_per_tile = LANES // ROPE_DIM
    for j in range(MLA_HEADS // 2):
        qa2 = _dot(q_nope[:, j * LANES:(j + 1) * LANES], wuk_ref[j])
        qcat_ref[2 * j, :, :KV_LORA] = qa2[:, :KV_LORA].astype(BF16)
        qcat_ref[2 * j + 1, :, :KV_LORA] = qa2[:, KV_LORA:].astype(BF16)
    for hd in range(MLA_HEADS):
        chunk = q_rope[:, (hd // heads_per_tile) * LANES:(hd // heads_per_tile + 1) * LANES]
        shift = (hd % heads_per_tile) * ROPE_DIM
        if shift:
            chunk = pltpu.roll(chunk, LANES - shift, axis=1)
        qcat_ref[hd, :, KV_LORA:] = jnp.where(lane < ROPE_DIM, chunk, 0.0).astype(BF16)

    yt = _dot_nt(wt_ref[...], h)
    tkt = tkt_ref[...]
    cm, sm = tkt[:MOBA_ROT // 2], tkt[MOBA_ROT // 2:]
    hr = MOBA_ROT // 2
    for hd in range(MOBA_HEADS):
        r0 = hd * MOBA_HD
        x1 = yt[r0:r0 + hr]
        x2 = yt[r0 + hr:r0 + 2 * hr]
        kt_ref[r0:r0 + hr, :] = x1 * cm - x2 * sm
        kt_ref[r0 + hr:r0 + 2 * hr, :] = x2 * cm + x1 * sm
        kt_ref[r0 + 2 * hr:r0 + MOBA_HD, :] = yt[r0 + 2 * hr:r0 + MOBA_HD]
    vt_ref[...] = yt[MOBA_W:2 * MOBA_W]
    trt = trt_ref[...]
    cr, sr = trt[:ROPE_DIM // 2], trt[ROPE_DIM // 2:]
    y1 = yt[2 * MOBA_W:2 * MOBA_W + ROPE_DIM // 2]
    y2 = yt[2 * MOBA_W + ROPE_DIM // 2:]
    krt_ref[...] = jnp.concatenate([y1 * cr - y2 * sr, y2 * cr + y1 * sr], axis=0)


def _proj(x2d, tables, tiles_per_seq, wts):
    n = x2d.shape[0]
    tm = ROW_TILE
    nseq = n // (tm * tiles_per_seq)
    seq_len = tm * tiles_per_seq
    t_qm, t_qr, t_kt, t_rt = tables
    g_attn, w_row, w_t, g_q, w_uq, g_kv, w_ukp = wts

    def full(a):
        nd = a.ndim
        return pl.BlockSpec(a.shape, lambda i: (0,) * nd)

    row = lambda w: pl.BlockSpec((tm, w), lambda i: (i, 0))
    tab_row = lambda w: pl.BlockSpec((tm, w), lambda i: (i % tiles_per_seq, 0))
    tab_col = lambda r: pl.BlockSpec((r, tm), lambda i: (0, i % tiles_per_seq))
    col = lambda r: pl.BlockSpec((None, r, tm), lambda i: (i // tiles_per_seq, 0, i % tiles_per_seq))

    out_shape = (
        jax.ShapeDtypeStruct((n, KV_LORA), F32),
        jax.ShapeDtypeStruct((n, MLA_KW), BF16),
        jax.ShapeDtypeStruct((nseq, tiles_per_seq, KV_LORA, tm), BF16),
        jax.ShapeDtypeStruct((nseq, ROPE_DIM, seq_len), F32),
        jax.ShapeDtypeStruct((nseq, MOBA_W, seq_len), F32),
        jax.ShapeDtypeStruct((nseq, MOBA_W, seq_len), F32),
        jax.ShapeDtypeStruct((n, MOBA_W), BF16),
        jax.ShapeDtypeStruct((nseq, MLA_HEADS, seq_len, MLA_KW), BF16),
        jax.ShapeDtypeStruct((n, D_MODEL), F32),
        jax.ShapeDtypeStruct((n, D_MODEL), F32),
    )
    out_specs = (
        row(KV_LORA), row(MLA_KW),
        pl.BlockSpec((None, None, KV_LORA, tm), lambda i: (i // tiles_per_seq, i % tiles_per_seq, 0, 0)),
        col(ROPE_DIM), col(MOBA_W), col(MOBA_W),
        row(MOBA_W),
        pl.BlockSpec((None, MLA_HEADS, tm, MLA_KW), lambda i: (i // tiles_per_seq, 0, i % tiles_per_seq, 0)),
        row(D_MODEL), row(D_MODEL),
    )
    in_specs = [row(D_MODEL), full(g_attn), full(w_row), full(w_t), full(g_q), full(w_uq), full(g_kv),
                full(w_ukp), tab_row(3 * LANES), tab_row(3 * LANES), tab_col(MOBA_ROT), tab_col(ROPE_DIM)]
    return pl.pallas_call(
        _proj_kernel, grid=(n // tm,), in_specs=in_specs, out_specs=out_specs, out_shape=out_shape,
        compiler_params=_cparams("arbitrary"), name="proj",
    )(x2d, g_attn, w_row, w_t, g_q, w_uq, g_kv, w_ukp, t_qm, t_qr, t_kt, t_rt)


def _prep_proj_weights(g_attn_norm, w_in, g_qnorm, w_uq, g_kvnorm, w_uk):
    o_q, o_kv, o_kr = 0, Q_LORA, Q_LORA + KV_LORA
    o_qm = o_kr + ROPE_DIM
    o_km, o_vm = o_qm + MOBA_W, o_qm + 2 * MOBA_W
    o_ga = o_vm + MOBA_W
    w_kr = jnp.pad(w_in[:, o_kr:o_qm], ((0, 0), (0, LANES - ROPE_DIM)))
    w_row = jnp.concatenate([w_in[:, o_q:o_kr], w_in[:, o_qm:o_km], w_kr, w_in[:, o_ga:]], axis=1).astype(BF16)
    w_t = jnp.concatenate([w_in[:, o_km:o_vm], w_in[:, o_vm:o_ga], w_in[:, o_kr:o_qm]], axis=1).T.astype(BF16)
    wq = w_uq.reshape(Q_LORA, MLA_HEADS, MLA_QK)
    w_uq_p = jnp.concatenate([wq[:, :, :NOPE_DIM].reshape(Q_LORA, -1),
                              wq[:, :, NOPE_DIM:].reshape(Q_LORA, -1)], axis=1).astype(BF16)
    wk = jnp.transpose(w_uk, (1, 2, 0)).astype(BF16)
    z = jnp.zeros((NOPE_DIM, KV_LORA), BF16)
    w_ukp = jnp.stack([
        jnp.concatenate([jnp.concatenate([wk[2 * j], z], axis=1),
                         jnp.concatenate([z, wk[2 * j + 1]], axis=1)], axis=0)
        for j in range(MLA_HEADS // 2)])
    return (g_attn_norm.reshape(1, -1), w_row, w_t, g_qnorm.reshape(1, -1), w_uq_p,
            g_kvnorm.reshape(1, -1), w_ukp)


ATT_TILE = 256


def _mla_prompt_kernel(q_ref, kcat_ref, ckvt_ref, wuvt_ref, o_ref, m_ref, l_ref, acc_ref):
    i = pl.program_id(1)
    t = ATT_TILE
    cols = MLA_HEADS * t
    q = q_ref[...].reshape(cols, MLA_KW)
    m_ref[...] = jnp.full(m_ref.shape, NEG_INF, F32)
    l_ref[...] = jnp.zeros(l_ref.shape, F32)
    acc_ref[...] = jnp.zeros(acc_ref.shape, F32)

    def step(j, mask):
        kc = kcat_ref[pl.ds(pl.multiple_of(j * t, t), t), :]
        st = _dot_nt(kc, q) * MLA_SCALE
        if mask is not None:
            st = jnp.where(mask, st, NEG_INF)
        m_old = m_ref[...]
        m_new = jnp.maximum(m_old, jnp.max(st, axis=0, keepdims=True))
        alpha = jnp.exp(m_old - m_new)
        p = jnp.exp(st - m_new)
        l_ref[...] = alpha * l_ref[...] + jnp.sum(p, axis=0, keepdims=True)
        acc_ref[...] = alpha * acc_ref[...] + _dot(ckvt_ref[j], p.astype(BF16))
        m_ref[...] = m_new

    def body(j, carry):
        step(j, None)
        return carry

    lax.fori_loop(0, i, body, 0)
    kpos = lax.broadcasted_iota(jnp.int32, (t, cols), 0)
    qpos = lax.broadcasted_iota(jnp.int32, (t, cols), 1) % t
    step(i, kpos <= qpos)

    o_lat = (acc_ref[...] / l_ref[...]).astype(BF16)
    o_t = jnp.concatenate([_dot(wuvt_ref[h], o_lat[:, h * t:(h + 1) * t]) for h in range(MLA_HEADS)], axis=0)
    o_ref[...] = o_t.T.astype(BF16)


def _prep_wuv(w_uv):
    w = jnp.transpose(w_uv, (1, 0, 2)).astype(BF16)
    z = jnp.zeros_like(w)
    even = jnp.concatenate([w, z], axis=2)
    odd = jnp.concatenate([z, w], axis=2)
    is_even = (jnp.arange(MLA_HEADS) % 2 == 0)[:, None, None]
    return jnp.where(is_even, even, odd)


def _prep_wuvt(w_uv):
    return jnp.transpose(w_uv, (1, 2, 0)).astype(BF16)


def _mla_prompt(qcat, kcat, ckvt, wuvt, b, s):
    t = ATT_TILE
    nq = s // t
    cols = MLA_HEADS * t
    return pl.pallas_call(
        _mla_prompt_kernel, grid=(b, nq),
        in_specs=[
            pl.BlockSpec((None, MLA_HEADS, t, MLA_KW), lambda bi, i: (bi, 0, i, 0)),
            pl.BlockSpec((s, MLA_KW), lambda bi, i: (bi, 0)),
            pl.BlockSpec((None, nq, KV_LORA, t), lambda bi, i: (bi, 0, 0, 0)),
            pl.BlockSpec(wuvt.shape, lambda bi, i: (0, 0, 0)),
        ],
        out_specs=pl.BlockSpec((t, MLA_W), lambda bi, i: (bi * nq + i, 0)),
        out_shape=jax.ShapeDtypeStruct((b * s, MLA_W), BF16),
        scratch_shapes=[pltpu.VMEM((1, cols), F32), pltpu.VMEM((1, cols), F32), pltpu.VMEM((KV_LORA, cols), F32)],
        compiler_params=_cparams("arbitrary", "arbitrary"), name="mla_p",
    )(qcat, kcat, ckvt, wuvt)


def _moba_attend(nb, q, krow_ref, vbf_ref, o_ref):
    t = MOBA_BLOCK
    nk = nb * t
    lane = lax.broadcasted_iota(jnp.int32, (t, LANES), 1)
    causal = lax.broadcasted_iota(jnp.int32, (t, t), 0) <= lax.broadcasted_iota(jnp.int32, (t, t), 1)
    n_past = nb - 1
    halves = []
    for half in range(2):
        qh = jnp.where((lane >= MOBA_HD) == bool(half), q, jnp.zeros_like(q))
        st = _dot_nt(krow_ref[0:nk, :], qh)
        blks = [st[n * t:(n + 1) * t] for n in range(nb)]
        blks[-1] = jnp.where(causal, blks[-1], NEG_INF)
        bmax = [jnp.max(b, axis=0, keepdims=True) for b in blks]
        mx = bmax[-1]
        if n_past > MOBA_TOPK:
            bsum = [jnp.sum(b, axis=0, keepdims=True) for b in blks[:-1]]
            keep = []
            for n in range(n_past):
                rank = jnp.zeros((1, t), F32)
                for m in range(n_past):
                    if m != n:
                        beats = (bsum[m] >= bsum[n]) if m < n else (bsum[m] > bsum[n])
                        rank = rank + beats.astype(F32)
                keep.append(rank < MOBA_TOPK)
                mx = jnp.maximum(mx, jnp.where(keep[n], bmax[n], NEG_INF))
        else:
            keep = [None] * n_past
            for n in range(n_past):
                mx = jnp.maximum(mx, bmax[n])
        l = jnp.zeros((1, t), F32)
        ps = []
        for n in range(nb):
            p = jnp.exp(blks[n] - mx)
            if n < n_past and keep[n] is not None:
                p = jnp.where(keep[n], p, 0.0)
            l = l + jnp.sum(p, axis=0, keepdims=True)
            ps.append(p.astype(BF16))
        pt = jnp.concatenate(ps, axis=0) if nb > 1 else ps[0]
        halves.append(_dot(vbf_ref[:, 0:nk], pt) / l)
    sub = lax.broadcasted_iota(jnp.int32, (LANES, t), 0)
    o_ref[...] = jnp.where(sub < MOBA_HD, halves[0], halves[1]).T.astype(BF16)


def _moba_prompt_kernel(nblk, q_ref, kt_ref, vt_ref, o_ref, krow_ref, vbf_ref):
    i = pl.program_id(2)

    @pl.when(i == 0)
    def _():
        krow_ref[...] = kt_ref[...].T.astype(BF16)
        vbf_ref[...] = vt_ref[...].astype(BF16)

    q = q_ref[...]
    for nb in range(1, nblk + 1):
        pl.when(i == nb - 1)(functools.partial(_moba_attend, nb, q, krow_ref, vbf_ref, o_ref))


def _moba_prompt(qm, kt, vt, b, s):
    t = MOBA_BLOCK
    nq = s // t
    npair = MOBA_W // LANES
    return pl.pallas_call(
        functools.partial(_moba_prompt_kernel, nq), grid=(b, npair, nq),
        in_specs=[
            pl.BlockSpec((t, LANES), lambda bi, j, i: (bi * nq + i, j)),
            pl.BlockSpec((None, LANES, s), lambda bi, j, i: (bi, j, 0)),
            pl.BlockSpec((None, LANES, s), lambda bi, j, i: (bi, j, 0)),
        ],
        out_specs=pl.BlockSpec((t, LANES), lambda bi, j, i: (bi * nq + i, j)),
        out_shape=jax.ShapeDtypeStruct((b * s, MOBA_W), BF16),
        scratch_shapes=[pltpu.VMEM((s, LANES), BF16), pltpu.VMEM((LANES, s), BF16)],
        compiler_params=_cparams("arbitrary", "arbitrary", "arbitrary"), name="moba_p",
    )(qm, kt, vt)


FF_CHUNK = D_FF // 2
HIST_ROWS = 8
SAMPLE_TILE = 128


def _tail_kernel(seq_tiles, seq_rows, *refs, hook=None):
    per_row_hist = seq_tiles == 0
    hook = hook or (lambda k: None)
    if per_row_hist:
        (x_ref, olat_ref, ob_ref, ga_ref, gb_ref, hist_ref, wuv_ref, wa_ref, wb_ref, wo_ref, gf_ref,
         wup_ref, cw_ref, cb_ref, wdn_ref, gfin_ref, y_ref, u_ref, ue_ref) = refs
        om = jnp.concatenate(
            [_dot(olat_ref[2 * j], wuv_ref[2 * j]) + _dot(olat_ref[2 * j + 1], wuv_ref[2 * j + 1])
             for j in range(MLA_HEADS // 2)], axis=1).astype(BF16)
    else:
        (x_ref, om_ref, ob_ref, ga_ref, gb_ref, wa_ref, wb_ref, wo_ref, gf_ref, wup_ref,
         cw_ref, cb_ref, wdn_ref, gfin_ref, y_ref, conv_ref, ue_ref, carry_ref) = refs
        om = om_ref[...]
    i = pl.program_id(0)
    tm = x_ref.shape[0]
    cw = FF_CHUNK

    a = _dot(om, wa_ref[...])
    b = _dot(ob_ref[...], wb_ref[...])
    mg = (ga_ref[...] * a + gb_ref[...] * b).astype(BF16)
    x1 = x_ref[...] + _dot(mg, wo_ref[...])
    hn = _rms(x1, gf_ref[...]).astype(BF16)

    if per_row_hist:
        tpos = lax.broadcasted_iota(jnp.int32, (tm, cw), 0) % seq_rows
        ue_ref[0:HIST_ROWS, :] = jnp.zeros((HIST_ROWS, cw), F32)
    else:
        first = (i % seq_tiles) == 0

        @pl.when(i == 0)
        def _():
            carry_ref[...] = jnp.zeros(carry_ref.shape, F32)

    hook(0)
    acc = x1
    for c in range(D_FF // cw):
        halves = []
        for half in range(2):
            if (c, half) == (1, 1):
                hook(2)
            c0 = half * D_FF + c * cw
            u = _dot(hn, wup_ref[:, c0:c0 + cw])
            if not per_row_hist:
                ue_ref[0:HIST_ROWS, :] = jnp.where(first, 0.0, carry_ref[:, c0:c0 + cw])
            ue_ref[HIST_ROWS:, :] = u
            u1 = ue_ref[HIST_ROWS - 1:HIST_ROWS - 1 + tm, :]
            u2 = ue_ref[HIST_ROWS - 2:HIST_ROWS - 2 + tm, :]
            if per_row_hist:
                hh = hist_ref[:, c0:c0 + cw]
                u1 = jnp.where(tpos < 1, pltpu.roll(hh, tm - 1, axis=0), u1)
                u2 = jnp.where(tpos < 2, hh, u2)
                u_ref[:, c0:c0 + cw] = u
            else:
                carry_ref[:, c0:c0 + cw] = u[tm - HIST_ROWS:, :]
                conv_ref[:, c0:c0 + cw] = u[tm - (CONV_W - 1):, :]
            halves.append(cb_ref[:, c0:c0 + cw] + cw_ref[0:1, c0:c0 + cw] * u2
                          + cw_ref[1:2, c0:c0 + cw] * u1 + cw_ref[2:3, c0:c0 + cw] * u)
        act = (jax.nn.silu(halves[0]) * halves[1]).astype(BF16)
        if c == 0:
            hook(1)
        acc = acc + _dot(act, wdn_ref[c * cw:(c + 1) * cw, :])
    y_ref[...] = _rms(acc, gfin_ref[...])
    hook(3)


def _tail(x2d, o_mla, o_moba, ga, gb, wts, seq_tiles, hist=None, seq_rows=0, wuv=None, tm=ROW_TILE):
    n = x2d.shape[0]
    row = lambda w: pl.BlockSpec((tm, w), lambda i: (i, 0))

    def const(a):
        nd = a.ndim
        return pl.BlockSpec(a.shape, lambda i: (0,) * nd, pipeline_mode=pl.Buffered(1))

    acts = [x2d, o_mla, o_moba, ga, gb]
    act_specs = [row(D_MODEL), row(MLA_W), row(MOBA_W), row(D_MODEL), row(D_MODEL)]
    scratch = [pltpu.VMEM((tm + HIST_ROWS, FF_CHUNK), F32)]
    if seq_tiles == 0:
        acts.append(hist)
        act_specs[1] = pl.BlockSpec((MLA_HEADS, tm, KV_LORA), lambda i: (0, i, 0))
        act_specs.append(row(2 * D_FF))
        wts = (wuv,) + tuple(wts)
        out_shape = (jax.ShapeDtypeStruct((n, D_MODEL), F32), jax.ShapeDtypeStruct((n, 2 * D_FF), F32))
        out_specs = (row(D_MODEL), row(2 * D_FF))
    else:
        nseq = n // (tm * seq_tiles)
        out_shape = (jax.ShapeDtypeStruct((n, D_MODEL), F32),
                     jax.ShapeDtypeStruct((nseq, CONV_W - 1, 2 * D_FF), F32))
        out_specs = (row(D_MODEL), pl.BlockSpec((None, CONV_W - 1, 2 * D_FF), lambda i: (i // seq_tiles, 0, 0)))
        scratch.append(pltpu.VMEM((HIST_ROWS, 2 * D_FF), F32))
    return pl.pallas_call(
        functools.partial(_tail_kernel, seq_tiles, seq_rows), grid=(n // tm,),
        in_specs=act_specs + [const(w) for w in wts], out_specs=out_specs, out_shape=out_shape,
        scratch_shapes=scratch, compiler_params=_cparams("arbitrary"), name="tail",
    )(*acts, *wts)


def _prep_tail_weights(w_br_mla, w_br_moba, w_out, g_ffn_norm, w_up, conv_w, conv_b, w_down, g_final):
    return (w_br_mla.astype(BF16), w_br_moba.astype(BF16), w_out.astype(BF16), g_ffn_norm.reshape(1, -1),
            w_up.astype(BF16), conv_w, conv_b.reshape(1, -1), w_down.astype(BF16), g_final.reshape(1, -1))


PAGE_CHUNK = 32


def _page_stream(pt_ref, hbm_bufs, sem, n_chunks, compute):
    d = pl.program_id(0)
    total = pl.num_programs(0) * n_chunks

    def copies(g, slot):
        dd = g // n_chunks
        c = g % n_chunks
        out = []
        for k in range(PAGE_CHUNK):
            page = pt_ref[dd, c * PAGE_CHUNK + k]
            for idx, (hbm, buf) in enumerate(hbm_bufs):
                out.append(pltpu.make_async_copy(hbm.at[page], buf.at[slot, k], sem.at[slot, idx]))
        return out

    @pl.when(d == 0)
    def _():
        for cp in copies(0, 0):
            cp.start()

    def body(c, carry):
        g = d * n_chunks + c
        slot = g % 2

        @pl.when(g + 1 < total)
        def _():
            for cp in copies(g + 1, 1 - slot):
                cp.start()

        for cp in copies(g, slot):
            cp.wait()
        compute(slot, c)
        return carry

    lax.fori_loop(0, n_chunks, body, 0)


def _mla_sample_kernel(n_chunks, pt_ref, qa_ref, qr_ref, cn_ref, krn_ref, lat_hbm, kr_hbm, o_ref,
                       lat_buf, kr_buf, sem, m_ref, l_ref, acc_ref):
    qa = qa_ref[...]
    qr = qr_ref[...]
    rows = qa.shape[0]
    n_new = cn_ref.shape[0]
    m_ref[...] = jnp.full(m_ref.shape, NEG_INF, F32)
    l_ref[...] = jnp.zeros(l_ref.shape, F32)
    acc_ref[...] = jnp.zeros(acc_ref.shape, F32)

    def compute(slot, c):
        del c
        cb = lat_buf[slot].reshape(PAGE_CHUNK * PAGE_SIZE, KV_LORA).astype(BF16)
        kr = jnp.concatenate([kr_buf[slot, k] for k in range(PAGE_CHUNK)], axis=1).astype(BF16)
        s = (_dot_nt(qa, cb) + _dot(qr, kr)) * MLA_SCALE
        m_old = m_ref[...]
        m_new = jnp.maximum(m_old, jnp.max(s, axis=1, keepdims=True))
        alpha = jnp.exp(m_old - m_new)
        p = jnp.exp(s - m_new)
        l_ref[...] = alpha * l_ref[...] + jnp.sum(p, axis=1, keepdims=True)
        acc_ref[...] = alpha * acc_ref[...] + _dot(p.astype(BF16), cb)
        m_ref[...] = m_new

    _page_stream(pt_ref, [(lat_hbm, lat_buf), (kr_hbm, kr_buf)], sem, n_chunks, compute)

    qaf = qa.astype(F32)
    qrf = qr.astype(F32)
    cn = cn_ref[...]
    krn = krn_ref[...]
    tq = lax.broadcasted_iota(jnp.int32, (rows, 1), 0) % n_new
    s_new = []
    for t in range(n_new):
        st = (jnp.sum(qaf * cn[t:t + 1], axis=1, keepdims=True)
              + jnp.sum(qrf * krn[t:t + 1], axis=1, keepdims=True)) * MLA_SCALE
        s_new.append(jnp.where(t <= tq, st, NEG_INF))
    m_old = m_ref[...]
    m_new = functools.reduce(jnp.maximum, s_new, m_old)
    alpha = jnp.exp(m_old - m_new)
    l = alpha * l_ref[...]
    acc = alpha * acc_ref[...]
    for t in range(n_new):
        p = jnp.exp(s_new[t] - m_new)
        l = l + p
        acc = acc + p * cn[t:t + 1]
    o_ref[...] = (acc / l).astype(o_ref.dtype)


def _mla_sample(page_table, qa_rows, qr_rows, c_new, kr_new, cache_lat, cache_kr_t):
    db, n_pages = page_table.shape
    n_chunks = n_pages // PAGE_CHUNK
    rows = qa_rows.shape[1]
    t_new = c_new.shape[1]
    per_d = lambda *shape: pl.BlockSpec((None,) + shape, lambda d, pt: (d,) + (0,) * len(shape))
    grid_spec = pltpu.PrefetchScalarGridSpec(
        num_scalar_prefetch=1, grid=(db,),
        in_specs=[per_d(rows, KV_LORA), per_d(rows, ROPE_DIM), per_d(t_new, KV_LORA), per_d(t_new, ROPE_DIM),
                  pl.BlockSpec(memory_space=pl.ANY), pl.BlockSpec(memory_space=pl.ANY)],
        out_specs=per_d(rows, KV_LORA),
        scratch_shapes=[pltpu.VMEM((2, PAGE_CHUNK, PAGE_SIZE, KV_LORA), F32),
                        pltpu.VMEM((2, PAGE_CHUNK, ROPE_DIM, PAGE_SIZE), F32),
                        pltpu.SemaphoreType.DMA((2, 2)),
                        pltpu.VMEM((rows, 1), F32), pltpu.VMEM((rows, 1), F32), pltpu.VMEM((rows, KV_LORA), F32)])
    return pl.pallas_call(
        functools.partial(_mla_sample_kernel, n_chunks), grid_spec=grid_spec,
        out_shape=jax.ShapeDtypeStruct((db, rows, KV_LORA), BF16),
        compiler_params=_cparams("arbitrary"), name="mla_s",
    )(page_table, qa_rows, qr_rows, c_new, kr_new, cache_lat, cache_kr_t)


PAGES_PER_BLOCK = MOBA_BLOCK // PAGE_SIZE


def _moba_sample_scores_kernel(n_chunks, pt_ref, q_ref, kn_ref, k_hbm, psel_ref, idx_ref, pown_ref,
                               k_buf, sem, s_ref):
    q = q_ref[...]
    _page_stream(pt_ref, [(k_hbm, k_buf)], sem, n_chunks,
                 lambda slot, c: _moba_scores_chunk(q, k_buf, slot, c, s_ref))
    _moba_scores_finish(n_chunks * PAGE_CHUNK, q, kn_ref[...], s_ref, psel_ref, idx_ref, pown_ref)


def _moba_scores_chunk(q, k_buf, slot, c, s_ref):
    for k in range(PAGE_CHUNK):
        s_ref[c * PAGE_CHUNK + k] = _dot(q, k_buf[slot, k].astype(BF16))


def _moba_scores_finish(n_pages, q, kn, s_ref, psel_ref, idx_ref, pown_ref):
    rows = q.shape[0]
    n_new = kn.shape[0]
    n_blocks = n_pages // PAGES_PER_BLOCK
    lane = lax.broadcasted_iota(jnp.int32, (rows, LANES), 1)
    blk = lambda n: jnp.concatenate([s_ref[PAGES_PER_BLOCK * n + k] for k in range(PAGES_PER_BLOCK)], axis=1)
    bsum = [jnp.sum(blk(n), axis=1, keepdims=True) for n in range(n_blocks)]
    bs = jnp.full((rows, LANES), NEG_INF, F32)
    for n in range(n_blocks):
        bs = jnp.where(lane == n, bsum[n], bs)
    rank = []
    for n in range(n_blocks):
        beats = jnp.logical_or(bs > bsum[n], jnp.logical_and(bs == bsum[n], lane < n))
        rank.append(jnp.sum(beats.astype(F32), axis=1, keepdims=True))
    s_sel = [jnp.zeros((rows, MOBA_BLOCK), F32) for _ in range(MOBA_TOPK)]
    idx = jnp.zeros((rows, LANES), F32)
    for n in range(n_blocks):
        b = blk(n)
        for r in range(MOBA_TOPK):
            hit = rank[n] == float(r)
            s_sel[r] = jnp.where(hit, b, s_sel[r])
            idx = jnp.where(jnp.logical_and(hit, lane == r), float(n), idx)

    qf = q.astype(F32)
    tq = lax.broadcasted_iota(jnp.int32, (rows, 1), 0) % n_new
    s_new = [jnp.where(t <= tq, jnp.sum(qf * kn[t:t + 1], axis=1, keepdims=True), NEG_INF) for t in range(n_new)]

    m = functools.reduce(jnp.maximum, s_new + [jnp.max(s, axis=1, keepdims=True) for s in s_sel])
    p_new = [jnp.exp(s - m) for s in s_new]
    p_sel = [jnp.exp(s - m) for s in s_sel]
    l = functools.reduce(lambda a, b: a + b, p_new + [jnp.sum(p, axis=1, keepdims=True) for p in p_sel])
    inv = 1.0 / l
    for r in range(MOBA_TOPK):
        psel_ref[:, r * MOBA_BLOCK:(r + 1) * MOBA_BLOCK] = p_sel[r] * inv
    idx_ref[...] = idx
    pown = jnp.zeros((rows, LANES), F32)
    for t in range(n_new):
        pown = jnp.where(lane == t, p_new[t] * inv, pown)
    pown_ref[...] = pown


SEL_KEYS = MOBA_TOPK * MOBA_BLOCK
TAIL_PHASES = 4


def _tail_scores_kernel(seq_tiles, n_chunks, pt_ref, x_ref, om_ref, ob_ref, ga_ref, gb_ref, q_ref, kn_ref, k_hbm,
                        wa_ref, wb_ref, wo_ref, gf_ref, wup_ref, cw_ref, cb_ref, wdn_ref, gfin_ref,
                        y_ref, conv_ref, psel_ref, idx_ref, pown_ref, ue_ref, carry_ref, k_buf, sem, s_ref):
    i = pl.program_id(0)
    total = pl.num_programs(0) * TAIL_PHASES
    assert q_ref.shape[0] * n_chunks == TAIL_PHASES

    def copies(g, slot):
        dd = g // n_chunks
        c = g % n_chunks
        return [pltpu.make_async_copy(k_hbm.at[pt_ref[dd, c * PAGE_CHUNK + k]], k_buf.at[slot, k], sem.at[slot])
                for k in range(PAGE_CHUNK)]

    @pl.when(i == 0)
    def _():
        for cp in copies(0, 0):
            cp.start()

    def hook(ph):
        g = i * TAIL_PHASES + ph
        slot = ph % 2

        @pl.when(g + 1 < total)
        def _():
            for cp in copies(g + 1, 1 - slot):
                cp.start()

        for cp in copies(g, slot):
            cp.wait()
        j, c = divmod(ph, n_chunks)
        _moba_scores_chunk(q_ref[j], k_buf, slot, c, s_ref)
        if c == n_chunks - 1:
            _moba_scores_finish(n_chunks * PAGE_CHUNK, q_ref[j], kn_ref[j], s_ref,
                                psel_ref.at[j], idx_ref.at[j], pown_ref.at[j])

    _tail_kernel(seq_tiles, 0, x_ref, om_ref, ob_ref, ga_ref, gb_ref, wa_ref, wb_ref, wo_ref, gf_ref, wup_ref,
                 cw_ref, cb_ref, wdn_ref, gfin_ref, y_ref, conv_ref, ue_ref, carry_ref, hook=hook)


def _tail_with_scores(x2d, o_mla, o_moba, ga, gb, wts, seq_tiles, page_table, qbd, kn, kc):
    n = x2d.shape[0]
    tm = ROW_TILE
    steps = n // tm
    db, n_pages = page_table.shape
    n_chunks = n_pages // PAGE_CHUNK
    sp = db // steps
    rows = qbd.shape[1]
    t_new = kn.shape[1]
    assert db % steps == 0 and sp * n_chunks == TAIL_PHASES
    row = lambda w: pl.BlockSpec((tm, w), lambda i, pt: (i, 0))
    seq = lambda *shape: pl.BlockSpec((sp,) + shape, lambda i, pt: (i,) + (0,) * len(shape))

    def const(a):
        nd = a.ndim
        return pl.BlockSpec(a.shape, lambda i, pt: (0,) * nd, pipeline_mode=pl.Buffered(1))

    nseq = n // (tm * seq_tiles)
    grid_spec = pltpu.PrefetchScalarGridSpec(
        num_scalar_prefetch=1, grid=(steps,),
        in_specs=[row(D_MODEL), row(MLA_W), row(MOBA_W), row(D_MODEL), row(D_MODEL),
                  seq(rows, MOBA_W), seq(t_new, MOBA_W), pl.BlockSpec(memory_space=pl.ANY)]
                 + [const(w) for w in wts],
        out_specs=(row(D_MODEL),
                   pl.BlockSpec((None, CONV_W - 1, 2 * D_FF), lambda i, pt: (i // seq_tiles, 0, 0)),
                   seq(rows, SEL_KEYS), seq(rows, LANES), seq(rows, LANES)),
        scratch_shapes=[pltpu.VMEM((tm + HIST_ROWS, FF_CHUNK), F32), pltpu.VMEM((HIST_ROWS, 2 * D_FF), F32),
                        pltpu.VMEM((2, PAGE_CHUNK, MOBA_W, PAGE_SIZE), F32), pltpu.SemaphoreType.DMA((2,)),
                        pltpu.VMEM((n_pages, rows, PAGE_SIZE), F32)])
    return pl.pallas_call(
        functools.partial(_tail_scores_kernel, seq_tiles, n_chunks), grid_spec=grid_spec,
        out_shape=(jax.ShapeDtypeStruct((n, D_MODEL), F32),
                   jax.ShapeDtypeStruct((nseq, CONV_W - 1, 2 * D_FF), F32),
                   jax.ShapeDtypeStruct((db, rows, SEL_KEYS), F32),
                   jax.ShapeDtypeStruct((db, rows, LANES), F32),
                   jax.ShapeDtypeStruct((db, rows, LANES), F32)),
        compiler_params=_cparams("arbitrary"), name="tail_scores",
    )(page_table, x2d, o_mla, o_moba, ga, gb, qbd, kn, kc, *wts)


def _moba_sample_pv_kernel(n_new, pt_ref, sel_ref, psel_ref, pown_ref, vn_ref, v_hbm, o_ref, v_buf, sem):
    d = pl.program_id(0)
    nd = pl.num_programs(0)
    rows = MOBA_HEADS * n_new

    def copies(dd, slot):
        out = []
        for h in range(MOBA_HEADS):
            for t in range(n_new):
                for r in range(MOBA_TOPK):
                    blk = sel_ref[dd, (h * n_new + t) * MOBA_TOPK + r]
                    for k in range(PAGES_PER_BLOCK):
                        page = pt_ref[dd, blk * PAGES_PER_BLOCK + k]
                        off = ((t * MOBA_TOPK + r) * PAGES_PER_BLOCK + k) * PAGE_SIZE
                        out.append(pltpu.make_async_copy(
                            v_hbm.at[page, h], v_buf.at[slot, h, :, pl.ds(off, PAGE_SIZE)], sem.at[slot]))
        return out

    @pl.when(d == 0)
    def _():
        for cp in copies(0, 0):
            cp.start()

    slot = d % 2

    @pl.when(d + 1 < nd)
    def _():
        for cp in copies(d + 1, 1 - slot):
            cp.start()

    for cp in copies(d, slot):
        cp.wait()

    width = n_new * SEL_KEYS
    p = psel_ref[...]
    seg = lax.broadcasted_iota(jnp.int32, (rows, width), 1) // SEL_KEYS
    tok = lax.broadcasted_iota(jnp.int32, (rows, width), 0) % n_new
    p_all = jnp.where(seg == tok, jnp.concatenate([p] * n_new, axis=1), 0.0).astype(BF16)
    pown = pown_ref[...]
    row_head = lax.broadcasted_iota(jnp.int32, (rows, MOBA_HD), 0) // n_new
    acc = jnp.zeros((rows, MOBA_HD), F32)
    for h in range(MOBA_HEADS):
        o_h = _dot_nt(p_all, v_buf[slot, h].astype(BF16))
        vn = vn_ref[h]
        for t in range(n_new):
            o_h = o_h + pown[:, t:t + 1] * vn[t:t + 1]
        acc = jnp.where(row_head == h, o_h, acc)
    o_ref[...] = acc.astype(o_ref.dtype)


def _moba_sample_prep(q, k_new, cache_k_t):
    db, t_new = q.shape[:2]
    rows = t_new * MOBA_HEADS
    head_mask = (jnp.arange(MOBA_HEADS)[:, None] == jnp.arange(MOBA_HEADS)[None, :])
    qht = jnp.transpose(q, (0, 2, 1, 3))
    qbd = jnp.where(head_mask[None, :, None, :, None], qht[:, :, :, None, :], 0.0)
    qbd = qbd.reshape(db, rows, MOBA_W).astype(BF16)
    return qbd, k_new.reshape(db, t_new, MOBA_W), cache_k_t.reshape(cache_k_t.shape[0], MOBA_W, PAGE_SIZE)


def _per_seq(*shape):
    return pl.BlockSpec((None,) + shape, lambda d, *_: (d,) + (0,) * len(shape))


def _moba_sample_scores(page_table, qbd, kn, kc):
    db, n_pages = page_table.shape
    n_chunks = n_pages // PAGE_CHUNK
    rows, t_new = qbd.shape[1], kn.shape[1]
    per_d = _per_seq
    return pl.pallas_call(
        functools.partial(_moba_sample_scores_kernel, n_chunks),
        grid_spec=pltpu.PrefetchScalarGridSpec(
            num_scalar_prefetch=1, grid=(db,),
            in_specs=[per_d(rows, MOBA_W), per_d(t_new, MOBA_W), pl.BlockSpec(memory_space=pl.ANY)],
            out_specs=(per_d(rows, SEL_KEYS), per_d(rows, LANES), per_d(rows, LANES)),
            scratch_shapes=[pltpu.VMEM((2, PAGE_CHUNK, MOBA_W, PAGE_SIZE), F32), pltpu.SemaphoreType.DMA((2, 1)),
                            pltpu.VMEM((n_pages, rows, PAGE_SIZE), F32)]),
        out_shape=(jax.ShapeDtypeStruct((db, rows, SEL_KEYS), F32),
                   jax.ShapeDtypeStruct((db, rows, LANES), F32),
                   jax.ShapeDtypeStruct((db, rows, LANES), F32)),
        compiler_params=_cparams("arbitrary"), name="moba_s_scores",
    )(page_table, qbd, kn, kc)


def _moba_sample_pv(page_table, psel, idx, pown, v_new, cache_v_t):
    db, rows = psel.shape[:2]
    t_new = v_new.shape[1]
    per_d = _per_seq
    vn = jnp.transpose(v_new, (0, 2, 1, 3))
    sel = idx[:, :, :MOBA_TOPK].astype(jnp.int32).reshape(db, rows * MOBA_TOPK)
    o = pl.pallas_call(
        functools.partial(_moba_sample_pv_kernel, t_new),
        grid_spec=pltpu.PrefetchScalarGridSpec(
            num_scalar_prefetch=2, grid=(db,),
            in_specs=[per_d(rows, SEL_KEYS), per_d(rows, LANES), per_d(MOBA_HEADS, t_new, MOBA_HD),
                      pl.BlockSpec(memory_space=pl.ANY)],
            out_specs=per_d(rows, MOBA_HD),
            scratch_shapes=[pltpu.VMEM((2, MOBA_HEADS, MOBA_HD, t_new * SEL_KEYS), F32),
                            pltpu.SemaphoreType.DMA((2,))]),
        out_shape=jax.ShapeDtypeStruct((db, rows, MOBA_HD), BF16),
        compiler_params=_cparams("arbitrary"), name="moba_s_pv",
    )(page_table, sel, psel, pown, vn, cache_v_t)
    return jnp.transpose(o.reshape(db, MOBA_HEADS, t_new, MOBA_HD), (0, 2, 1, 3)).reshape(db, t_new, MOBA_W)


def kernel(x_prompt, x_sample, cache_mla_latent, cache_mla_krope, cache_moba_k, cache_moba_v,
           state_ffn_conv, page_table, g_attn_norm, w_in, g_qnorm, w_uq, g_kvnorm, w_uk, w_uv,
           w_br_mla, w_br_moba, w_out, g_ffn_norm, w_up, conv_w, conv_b, w_down, g_final):
    b, s, d_model = x_prompt.shape
    db, t_new, _ = x_sample.shape
    assert d_model == D_MODEL and s % ROW_TILE == 0 and (db * t_new) % ROW_TILE == 0
    assert page_table.shape == (db, PAST_LEN // PAGE_SIZE) and ROW_TILE == ATT_TILE == MOBA_BLOCK
    assert PAST_LEN % MOBA_BLOCK == 0 and ROW_TILE % t_new == 0

    pw = _prep_proj_weights(g_attn_norm, w_in, g_qnorm, w_uq, g_kvnorm, w_uk)
    tw = _prep_tail_weights(w_br_mla, w_br_moba, w_out, g_ffn_norm, w_up, conv_w, conv_b, w_down, g_final)
    wuv = _prep_wuv(w_uv)

    xp = x_prompt.reshape(b * s, D_MODEL)
    tabs_p = _rope_tables(jnp.arange(s, dtype=jnp.int32))
    ckv, kcat, ckvt, krt, kt, vt, qm, qcat, ga, gb = _proj(xp, tabs_p, s // ROW_TILE, pw)
    o_mla = _mla_prompt(qcat, kcat, ckvt, _prep_wuvt(w_uv), b, s)
    o_moba = _moba_prompt(qm, kt, vt, b, s)
    c_p = ckv.reshape(b, s, KV_LORA)
    kr_p = jnp.transpose(krt, (0, 2, 1))
    k_p = jnp.transpose(kt.reshape(b, MOBA_HEADS, MOBA_HD, s), (0, 3, 1, 2))
    v_p = jnp.transpose(vt.reshape(b, MOBA_HEADS, MOBA_HD, s), (0, 3, 1, 2))

    n_s = db * t_new
    xs = x_sample.reshape(n_s, D_MODEL)
    pos_s = jnp.tile(PAST_LEN + jnp.arange(t_new, dtype=jnp.int32), db)
    tabs_s = _rope_tables(pos_s)
    ckv_s, _, _, krt_s, kt_s, vt_s, qm_s, qcat_s, ga_s, gb_s = _proj(xs, tabs_s, n_s // ROW_TILE, pw)
    c_s = ckv_s.reshape(db, t_new, KV_LORA)
    kr_s = krt_s[0].T.reshape(db, t_new, ROPE_DIM)
    k_s = kt_s[0].T.reshape(db, t_new, MOBA_HEADS, MOBA_HD)
    v_s = vt_s[0].T.reshape(db, t_new, MOBA_HEADS, MOBA_HD)
    q_rows = jnp.transpose(qcat_s[0].reshape(MLA_HEADS, db, t_new, MLA_KW), (1, 0, 2, 3))
    q_rows = q_rows.reshape(db, MLA_HEADS * t_new, MLA_KW)
    qa_rows = q_rows[..., :KV_LORA]
    qr_rows = q_rows[..., KV_LORA:KV_LORA + ROPE_DIM]
    o_lat_s = _mla_sample(page_table, qa_rows, qr_rows, c_s, kr_s, cache_mla_latent,
                          jnp.transpose(cache_mla_krope, (0, 2, 1)))
    o_lat_s = jnp.transpose(o_lat_s.reshape(db, MLA_HEADS, t_new, KV_LORA), (1, 0, 2, 3))
    o_lat_s = o_lat_s.reshape(MLA_HEADS, n_s, KV_LORA)
    qbd, kn, kc = _moba_sample_prep(qm_s.reshape(db, t_new, MOBA_HEADS, MOBA_HD), k_s,
                                    jnp.transpose(cache_moba_k, (0, 2, 3, 1)))
    steps = (b * s) // ROW_TILE
    if db % steps == 0 and (db // steps) * (page_table.shape[1] // PAGE_CHUNK) == TAIL_PHASES:
        y_p, conv_p, psel, idx, pown = _tail_with_scores(xp, o_mla, o_moba, ga, gb, tw, s // ROW_TILE,
                                                         page_table, qbd, kn, kc)
    else:
        y_p, conv_p = _tail(xp, o_mla, o_moba, ga, gb, tw, s // ROW_TILE)
        psel, idx, pown = _moba_sample_scores(page_table, qbd, kn, kc)
    o_moba_s = _moba_sample_pv(page_table, psel, idx, pown, v_s, jnp.transpose(cache_moba_v, (0, 2, 3, 1)))
    hist = jnp.concatenate([state_ffn_conv, jnp.zeros((db, t_new - (CONV_W - 1), 2 * D_FF), F32)], axis=1)
    y_s, u_s = _tail(xs, o_lat_s, o_moba_s.reshape(n_s, MOBA_W), ga_s, gb_s, tw, 0,
                     hist=hist.reshape(n_s, -1), seq_rows=t_new, wuv=wuv, tm=SAMPLE_TILE)
    conv_s = u_s.reshape(db, t_new, 2 * D_FF)[:, t_new - (CONV_W - 1):]

    return (y_p.reshape(b, s, D_MODEL), y_s.reshape(db, t_new, D_MODEL), c_p, kr_p, k_p, v_p, conv_p,
            c_s, kr_s, k_s, v_s, conv_s)
```

```python
import functools

import jax
import jax.numpy as jnp
import numpy as np
from jax import lax
from jax.experimental import pallas as pl
from jax.experimental.pallas import tpu as pltpu

D_MODEL = 1024
PAST_LEN = 8192
PAGE_SIZE = 128

MLA_HEADS = 8
Q_LORA = 512
KV_LORA = 256
NOPE_DIM = 64
ROPE_DIM = 32
V_DIM = 64
MLA_QK = NOPE_DIM + ROPE_DIM
MLA_W = MLA_HEADS * V_DIM
MLA_SCALE = MLA_QK ** -0.5
MLA_THETA = 10000.0

MOBA_HEADS = 8
MOBA_HD = 64
MOBA_W = MOBA_HEADS * MOBA_HD
MOBA_ROT = MOBA_HD // 4
MOBA_BLOCK = 256
MOBA_TOPK = 3
MOBA_SCALE = MOBA_HD ** -0.5
ROPE_THETA = 500000.0

D_FF = 2816
CONV_W = 3
EPS = 1e-6

LANES = 128
ROW_TILE = 256
VMEM_LIMIT = 56 * 1024 * 1024
NEG_INF = float("-inf")

BF16 = jnp.bfloat16
F32 = jnp.float32


def _cparams(*sem):
    return pltpu.CompilerParams(dimension_semantics=sem, vmem_limit_bytes=VMEM_LIMIT)


def _dot(a, b):
    return jnp.dot(a, b, preferred_element_type=F32)


def _dot_nt(a, b):
    return lax.dot_general(a, b, (((1,), (1,)), ((), ())), preferred_element_type=F32)


def _rms(x, g):
    return x * lax.rsqrt(jnp.mean(x * x, axis=-1, keepdims=True) + EPS) * g


def _angles(pos, d, theta):
    half = d // 2
    inv = 1.0 / (theta ** (jnp.arange(half, dtype=F32) * (2.0 / d)))
    ang = pos.astype(F32)[:, None] * inv[None, :]
    return jnp.cos(ang), jnp.sin(ang)


def _row_table(cos, sin, group, lanes):
    half = cos.shape[1]
    p = cos.shape[0]
    lane = np.arange(lanes) % group
    first = lane < half
    second = (lane >= half) & (lane < 2 * half)
    idx = np.where(first, lane, np.where(second, lane - half, 0))
    c = jnp.where((first | second)[None, :], cos[:, idx], 1.0)
    sa = jnp.where(first[None, :], -sin[:, idx], 0.0)
    sb = jnp.where(second[None, :], sin[:, idx], 0.0)
    del p
    return jnp.concatenate([c, sa, sb], axis=1).astype(F32)


def _rope_tables(pos):
    cm, sm = _angles(pos, MOBA_ROT, ROPE_THETA)
    cr, sr = _angles(pos, ROPE_DIM, MLA_THETA)
    t_qm = _row_table(cm, sm, MOBA_HD, LANES)
    t_qr = _row_table(cr, sr, ROPE_DIM, LANES)
    t_kt = jnp.concatenate([cm.T, sm.T], axis=0)
    t_rt = jnp.concatenate([cr.T, sr.T], axis=0)
    return t_qm, t_qr, t_kt, t_rt


def _rope_rows(x, tab, half):
    c, sa, sb = tab[:, :LANES], tab[:, LANES:2 * LANES], tab[:, 2 * LANES:]
    outs = []
    for k in range(x.shape[1] // LANES):
        xk = x[:, k * LANES:(k + 1) * LANES]
        up = pltpu.roll(xk, LANES - half, axis=1)
        dn = pltpu.roll(xk, half, axis=1)
        outs.append(xk * c + up * sa + dn * sb)
    return jnp.concatenate(outs, axis=1) if len(outs) > 1 else outs[0]


W_T_ROWS = 2 * MOBA_W + ROPE_DIM
MLA_KW = KV_LORA + LANES


def _proj_kernel(x_ref, g_ref, wrow_ref, wt_ref, gq_ref, wuq_ref, gkv_ref, wuk_ref,
                 tqm_ref, tqr_ref, tkt_ref, trt_ref,
                 ckv_ref, kcat_ref, ckvt_ref, krt_ref, kt_ref, vt_ref,
                 qm_ref, qcat_ref, ga_ref, gb_ref):
    x = x_ref[...]
    h = _rms(x, g_ref[...]).astype(BF16)

    o = 0
    q_lat = _dot(h, wrow_ref[:, o:o + Q_LORA]); o += Q_LORA
    kv_lat = _dot(h, wrow_ref[:, o:o + KV_LORA]); o += KV_LORA
    q_m = _dot(h, wrow_ref[:, o:o + MOBA_W]); o += MOBA_W
    k_r = _dot(h, wrow_ref[:, o:o + LANES]); o += LANES
    ga_ref[...] = jax.nn.sigmoid(_dot(h, wrow_ref[:, o:o + D_MODEL])); o += D_MODEL
    gb_ref[...] = jax.nn.sigmoid(_dot(h, wrow_ref[:, o:o + D_MODEL]))

    ckv = _rms(kv_lat, gkv_ref[...])
    ckv_ref[...] = ckv
    kcat_ref[:, :KV_LORA] = ckv.astype(BF16)
    kcat_ref[:, KV_LORA:] = _rope_rows(k_r, tqr_ref[...], ROPE_DIM // 2).astype(BF16)
    ckvt_ref[...] = ckv.T.astype(BF16)

    qm_ref[...] = (_rope_rows(q_m, tqm_ref[...], MOBA_ROT // 2) * MOBA_SCALE).astype(BF16)

    qn = _rms(q_lat, gq_ref[...]).astype(BF16)
    q = _dot(qn, wuq_ref[...])
    q_nope = q[:, :MLA_HEADS * NOPE_DIM].astype(BF16)
    q_rope = _rope_rows(q[:, MLA_HEADS * NOPE_DIM:], tqr_ref[...], ROPE_DIM // 2)
    lane = lax.broadcasted_iota(jnp.int32, (q_rope.shape[0], LANES), 1)
    heads_per_tile = LANES // ROPE_DIM
    for j in range(MLA_HEADS // 2):
        qa2 = _dot(q_nope[:, j * LANES:(j + 1) * LANES], wuk_ref[j])
        qcat_ref[2 * j, :, :KV_LORA] = qa2[:, :KV_LORA].astype(BF16)
        qcat_ref[2 * j + 1, :, :KV_LORA] = qa2[:, KV_LORA:].astype(BF16)
    for hd in range(MLA_HEADS):
        chunk = q_rope[:, (hd // heads_per_tile) * LANES:(hd // heads_per_tile + 1) * LANES]
        shift = (hd % heads_per_tile) * ROPE_DIM
        if shift:
            chunk = pltpu.roll(chunk, LANES - shift, axis=1)
        qcat_ref[hd, :, KV_LORA:] = jnp.where(lane < ROPE_DIM, chunk, 0.0).astype(BF16)

    yt = _dot_nt(wt_ref[...], h)
    tkt = tkt_ref[...]
    cm, sm = tkt[:MOBA_ROT // 2], tkt[MOBA_ROT // 2:]
    hr = MOBA_ROT // 2
    for hd in range(MOBA_HEADS):
        r0 = hd * MOBA_HD
        x1 = yt[r0:r0 + hr]
        x2 = yt[r0 + hr:r0 + 2 * hr]
        kt_ref[r0:r0 + hr, :] = x1 * cm - x2 * sm
        kt_ref[r0 + hr:r0 + 2 * hr, :] = x2 * cm + x1 * sm
        kt_ref[r0 + 2 * hr:r0 + MOBA_HD, :] = yt[r0 + 2 * hr:r0 + MOBA_HD]
    vt_ref[...] = yt[MOBA_W:2 * MOBA_W]
    trt = trt_ref[...]
    cr, sr = trt[:ROPE_DIM // 2], trt[ROPE_DIM // 2:]
    y1 = yt[2 * MOBA_W:2 * MOBA_W + ROPE_DIM // 2]
    y2 = yt[2 * MOBA_W + ROPE_DIM // 2:]
    krt_ref[...] = jnp.concatenate([y1 * cr - y2 * sr, y2 * cr + y1 * sr], axis=0)


def _proj(x2d, tables, tiles_per_seq, wts):
    n = x2d.shape[0]
    tm = ROW_TILE
    nseq = n // (tm * tiles_per_seq)
    seq_len = tm * tiles_per_seq
    t_qm, t_qr, t_kt, t_rt = tables
    g_attn, w_row, w_t, g_q, w_uq, g_kv, w_ukp = wts

    def full(a):
        nd = a.ndim
        return pl.BlockSpec(a.shape, lambda i: (0,) * nd)

    row = lambda w: pl.BlockSpec((tm, w), lambda i: (i, 0))
    tab_row = lambda w: pl.BlockSpec((tm, w), lambda i: (i % tiles_per_seq, 0))
    tab_col = lambda r: pl.BlockSpec((r, tm), lambda i: (0, i % tiles_per_seq))
    col = lambda r: pl.BlockSpec((None, r, tm), lambda i: (i // tiles_per_seq, 0, i % tiles_per_seq))

    out_shape = (
        jax.ShapeDtypeStruct((n, KV_LORA), F32),
        jax.ShapeDtypeStruct((n, MLA_KW), BF16),
        jax.ShapeDtypeStruct((nseq, tiles_per_seq, KV_LORA, tm), BF16),
        jax.ShapeDtypeStruct((nseq, ROPE_DIM, seq_len), F32),
        jax.ShapeDtypeStruct((nseq, MOBA_W, seq_len), F32),
        jax.ShapeDtypeStruct((nseq, MOBA_W, seq_len), F32),
        jax.ShapeDtypeStruct((n, MOBA_W), BF16),
        jax.ShapeDtypeStruct((nseq, MLA_HEADS, seq_len, MLA_KW), BF16),
        jax.ShapeDtypeStruct((n, D_MODEL), F32),
        jax.ShapeDtypeStruct((n, D_MODEL), F32),
    )
    out_specs = (
        row(KV_LORA), row(MLA_KW),
        pl.BlockSpec((None, None, KV_LORA, tm), lambda i: (i // tiles_per_seq, i % tiles_per_seq, 0, 0)),
        col(ROPE_DIM), col(MOBA_W), col(MOBA_W),
        row(MOBA_W),
        pl.BlockSpec((None, MLA_HEADS, tm, MLA_KW), lambda i: (i // tiles_per_seq, 0, i % tiles_per_seq, 0)),
        row(D_MODEL), row(D_MODEL),
    )
    in_specs = [row(D_MODEL), full(g_attn), full(w_row), full(w_t), full(g_q), full(w_uq), full(g_kv),
                full(w_ukp), tab_row(3 * LANES), tab_row(3 * LANES), tab_col(MOBA_ROT), tab_col(ROPE_DIM)]
    return pl.pallas_call(
        _proj_kernel, grid=(n // tm,), in_specs=in_specs, out_specs=out_specs, out_shape=out_shape,
        compiler_params=_cparams("arbitrary"), name="proj",
    )(x2d, g_attn, w_row, w_t, g_q, w_uq, g_kv, w_ukp, t_qm, t_qr, t_kt, t_rt)


def _prep_proj_weights(g_attn_norm, w_in, g_qnorm, w_uq, g_kvnorm, w_uk):
    o_q, o_kv, o_kr = 0, Q_LORA, Q_LORA + KV_LORA
    o_qm = o_kr + ROPE_DIM
    o_km, o_vm = o_qm + MOBA_W, o_qm + 2 * MOBA_W
    o_ga = o_vm + MOBA_W
    w_kr = jnp.pad(w_in[:, o_kr:o_qm], ((0, 0), (0, LANES - ROPE_DIM)))
    w_row = jnp.concatenate([w_in[:, o_q:o_kr], w_in[:, o_qm:o_km], w_kr, w_in[:, o_ga:]], axis=1).astype(BF16)
    w_t = jnp.concatenate([w_in[:, o_km:o_vm], w_in[:, o_vm:o_ga], w_in[:, o_kr:o_qm]], axis=1).T.astype(BF16)
    wq = w_uq.reshape(Q_LORA, MLA_HEADS, MLA_QK)
    w_uq_p = jnp.concatenate([wq[:, :, :NOPE_DIM].reshape(Q_LORA, -1),
                              wq[:, :, NOPE_DIM:].reshape(Q_LORA, -1)], axis=1).astype(BF16)
    wk = jnp.transpose(w_uk, (1, 2, 0)).astype(BF16)
    z = jnp.zeros((NOPE_DIM, KV_LORA), BF16)
    w_ukp = jnp.stack([
        jnp.concatenate([jnp.concatenate([wk[2 * j], z], axis=1),
                         jnp.concatenate([z, wk[2 * j + 1]], axis=1)], axis=0)
        for j in range(MLA_HEADS // 2)])
    return (g_attn_norm.reshape(1, -1), w_row, w_t, g_qnorm.reshape(1, -1), w_uq_p,
            g_kvnorm.reshape(1, -1), w_ukp)


ATT_TILE = 256


def _mla_prompt_kernel(q_ref, kcat_ref, ckvt_ref, wuvt_ref, o_ref, m_ref, l_ref, acc_ref):
    i = pl.program_id(1)
    t = ATT_TILE
    cols = MLA_HEADS * t
    q = q_ref[...].reshape(cols, MLA_KW)
    m_ref[...] = jnp.full(m_ref.shape, NEG_INF, F32)
    l_ref[...] = jnp.zeros(l_ref.shape, F32)
    acc_ref[...] = jnp.zeros(acc_ref.shape, F32)

    def step(j, mask):
        kc = kcat_ref[pl.ds(pl.multiple_of(j * t, t), t), :]
        st = _dot_nt(kc, q) * MLA_SCALE
        if mask is not None:
            st = jnp.where(mask, st, NEG_INF)
        m_old = m_ref[...]
        m_new = jnp.maximum(m_old, jnp.max(st, axis=0, keepdims=True))
        alpha = jnp.exp(m_old - m_new)
        p = jnp.exp(st - m_new)
        l_ref[...] = alpha * l_ref[...] + jnp.sum(p, axis=0, keepdims=True)
        acc_ref[...] = alpha * acc_ref[...] + _dot(ckvt_ref[j], p.astype(BF16))
        m_ref[...] = m_new

    def body(j, carry):
        step(j, None)
        return carry

    lax.fori_loop(0, i, body, 0)
    kpos = lax.broadcasted_iota(jnp.int32, (t, cols), 0)
    qpos = lax.broadcasted_iota(jnp.int32, (t, cols), 1) % t
    step(i, kpos <= qpos)

    o_lat = (acc_ref[...] / l_ref[...]).astype(BF16)
    o_t = jnp.concatenate([_dot(wuvt_ref[h], o_lat[:, h * t:(h + 1) * t]) for h in range(MLA_HEADS)], axis=0)
    o_ref[...] = o_t.T.astype(BF16)


def _prep_wuv(w_uv):
    w = jnp.transpose(w_uv, (1, 0, 2)).astype(BF16)
    z = jnp.zeros_like(w)
    even = jnp.concatenate([w, z], axis=2)
    odd = jnp.concatenate([z, w], axis=2)
    is_even = (jnp.arange(MLA_HEADS) % 2 == 0)[:, None, None]
    return jnp.where(is_even, even, odd)


def _prep_wuvt(w_uv):
    return jnp.transpose(w_uv, (1, 2, 0)).astype(BF16)


def _mla_prompt(qcat, kcat, ckvt, wuvt, b, s):
    t = ATT_TILE
    nq = s // t
    cols = MLA_HEADS * t
    return pl.pallas_call(
        _mla_prompt_kernel, grid=(b, nq),
        in_specs=[
            pl.BlockSpec((None, MLA_HEADS, t, MLA_KW), lambda bi, i: (bi, 0, i, 0)),
            pl.BlockSpec((s, MLA_KW), lambda bi, i: (bi, 0)),
            pl.BlockSpec((None, nq, KV_LORA, t), lambda bi, i: (bi, 0, 0, 0)),
            pl.BlockSpec(wuvt.shape, lambda bi, i: (0, 0, 0)),
        ],
        out_specs=pl.BlockSpec((t, MLA_W), lambda bi, i: (bi * nq + i, 0)),
        out_shape=jax.ShapeDtypeStruct((b * s, MLA_W), BF16),
        scratch_shapes=[pltpu.VMEM((1, cols), F32), pltpu.VMEM((1, cols), F32), pltpu.VMEM((KV_LORA, cols), F32)],
        compiler_params=_cparams("arbitrary", "arbitrary"), name="mla_p",
    )(qcat, kcat, ckvt, wuvt)


def _moba_attend(nb, q, krow_ref, vbf_ref, o_ref):
    t = MOBA_BLOCK
    nk = nb * t
    lane = lax.broadcasted_iota(jnp.int32, (t, LANES), 1)
    causal = lax.broadcasted_iota(jnp.int32, (t, t), 0) <= lax.broadcasted_iota(jnp.int32, (t, t), 1)
    n_past = nb - 1
    halves = []
    for half in range(2):
        qh = jnp.where((lane >= MOBA_HD) == bool(half), q, jnp.zeros_like(q))
        st = _dot_nt(krow_ref[0:nk, :], qh)
        blks = [st[n * t:(n + 1) * t] for n in range(nb)]
        blks[-1] = jnp.where(causal, blks[-1], NEG_INF)
        bmax = [jnp.max(b, axis=0, keepdims=True) for b in blks]
        mx = bmax[-1]
        if n_past > MOBA_TOPK:
            bsum = [jnp.sum(b, axis=0, keepdims=True) for b in blks[:-1]]
            keep = []
            for n in range(n_past):
                rank = jnp.zeros((1, t), F32)
                for m in range(n_past):
                    if m != n:
                        beats = (bsum[m] >= bsum[n]) if m < n else (bsum[m] > bsum[n])
                        rank = rank + beats.astype(F32)
                keep.append(rank < MOBA_TOPK)
                mx = jnp.maximum(mx, jnp.where(keep[n], bmax[n], NEG_INF))
        else:
            keep = [None] * n_past
            for n in range(n_past):
                mx = jnp.maximum(mx, bmax[n])
        l = jnp.zeros((1, t), F32)
        ps = []
        for n in range(nb):
            p = jnp.exp(blks[n] - mx)
            if n < n_past and keep[n] is not None:
                p = jnp.where(keep[n], p, 0.0)
            l = l + jnp.sum(p, axis=0, keepdims=True)
            ps.append(p.astype(BF16))
        pt = jnp.concatenate(ps, axis=0) if nb > 1 else ps[0]
        halves.append(_dot(vbf_ref[:, 0:nk], pt) / l)
    sub = lax.broadcasted_iota(jnp.int32, (LANES, t), 0)
    o_ref[...] = jnp.where(sub < MOBA_HD, halves[0], halves[1]).T.astype(BF16)


def _moba_prompt_kernel(nblk, q_ref, kt_ref, vt_ref, o_ref, krow_ref, vbf_ref):
    i = pl.program_id(2)

    @pl.when(i == 0)
    def _():
        krow_ref[...] = kt_ref[...].T.astype(BF16)
        vbf_ref[...] = vt_ref[...].astype(BF16)

    q = q_ref[...]
    for nb in range(1, nblk + 1):
        pl.when(i == nb - 1)(functools.partial(_moba_attend, nb, q, krow_ref, vbf_ref, o_ref))


def _moba_prompt(qm, kt, vt, b, s):
    t = MOBA_BLOCK
    nq = s // t
    npair = MOBA_W // LANES
    return pl.pallas_call(
        functools.partial(_moba_prompt_kernel, nq), grid=(b, npair, nq),
        in_specs=[
            pl.BlockSpec((t, LANES), lambda bi, j, i: (bi * nq + i, j)),
            pl.BlockSpec((None, LANES, s), lambda bi, j, i: (bi, j, 0)),
            pl.BlockSpec((None, LANES, s), lambda bi, j, i: (bi, j, 0)),
        ],
        out_specs=pl.BlockSpec((t, LANES), lambda bi, j, i: (bi * nq + i, j)),
        out_shape=jax.ShapeDtypeStruct((b * s, MOBA_W), BF16),
        scratch_shapes=[pltpu.VMEM((s, LANES), BF16), pltpu.VMEM((LANES, s), BF16)],
        compiler_params=_cparams("arbitrary", "arbitrary", "arbitrary"), name="moba_p",
    )(qm, kt, vt)


FF_CHUNK = D_FF // 2
HIST_ROWS = 8
SAMPLE_TILE = 128


def _tail_kernel(seq_tiles, seq_rows, *refs, hook=None):
    per_row_hist = seq_tiles == 0
    hook = hook or (lambda k: None)
    if per_row_hist:
        (x_ref, olat_ref, ob_ref, ga_ref, gb_ref, hist_ref, wuv_ref, wa_ref, wb_ref, wo_ref, gf_ref,
         wup_ref, cw_ref, cb_ref, wdn_ref, gfin_ref, y_ref, u_ref, ue_ref) = refs
        om = jnp.concatenate(
            [_dot(olat_ref[2 * j], wuv_ref[2 * j]) + _dot(olat_ref[2 * j + 1], wuv_ref[2 * j + 1])
             for j in range(MLA_HEADS // 2)], axis=1).astype(BF16)
    else:
        (x_ref, om_ref, ob_ref, ga_ref, gb_ref, wa_ref, wb_ref, wo_ref, gf_ref, wup_ref,
         cw_ref, cb_ref, wdn_ref, gfin_ref, y_ref, conv_ref, ue_ref, carry_ref) = refs
        om = om_ref[...]
    i = pl.program_id(0)
    tm = x_ref.shape[0]
    cw = FF_CHUNK

    a = _dot(om, wa_ref[...])
    b = _dot(ob_ref[...], wb_ref[...])
    mg = (ga_ref[...] * a + gb_ref[...] * b).astype(BF16)
    x1 = x_ref[...] + _dot(mg, wo_ref[...])
    hn = _rms(x1, gf_ref[...]).astype(BF16)

    if per_row_hist:
        tpos = lax.broadcasted_iota(jnp.int32, (tm, cw), 0) % seq_rows
        ue_ref[0:HIST_ROWS, :] = jnp.zeros((HIST_ROWS, cw), F32)
    else:
        first = (i % seq_tiles) == 0

        @pl.when(i == 0)
        def _():
            carry_ref[...] = jnp.zeros(carry_ref.shape, F32)

    hook(0)
    acc = x1
    for c in range(D_FF // cw):
        halves = []
        for half in range(2):
            if (c, half) == (1, 1):
                hook(2)
            c0 = half * D_FF + c * cw
            u = _dot(hn, wup_ref[:, c0:c0 + cw])
            if not per_row_hist:
                ue_ref[0:HIST_ROWS, :] = jnp.where(first, 0.0, carry_ref[:, c0:c0 + cw])
            ue_ref[HIST_ROWS:, :] = u
            u1 = ue_ref[HIST_ROWS - 1:HIST_ROWS - 1 + tm, :]
            u2 = ue_ref[HIST_ROWS - 2:HIST_ROWS - 2 + tm, :]
            if per_row_hist:
                hh = hist_ref[:, c0:c0 + cw]
                u1 = jnp.where(tpos < 1, pltpu.roll(hh, tm - 1, axis=0), u1)
                u2 = jnp.where(tpos < 2, hh, u2)
                u_ref[:, c0:c0 + cw] = u
            else:
                carry_ref[:, c0:c0 + cw] = u[tm - HIST_ROWS:, :]
                conv_ref[:, c0:c0 + cw] = u[tm - (CONV_W - 1):, :]
            halves.append(cb_ref[:, c0:c0 + cw] + cw_ref[0:1, c0:c0 + cw] * u2
                          + cw_ref[1:2, c0:c0 + cw] * u1 + cw_ref[2:3, c0:c0 + cw] * u)
        act = (jax.nn.silu(halves[0]) * halves[1]).astype(BF16)
        if c == 0:
            hook(1)
        acc = acc + _dot(act, wdn_ref[c * cw:(c + 1) * cw, :])
    y_ref[...] = _rms(acc, gfin_ref[...])
    hook(3)


def _tail(x2d, o_mla, o_moba, ga, gb, wts, seq_tiles, hist=None, seq_rows=0, wuv=None, tm=ROW_TILE):
    n = x2d.shape[0]
    row = lambda w: pl.BlockSpec((tm, w), lambda i: (i, 0))

    def const(a):
        nd = a.ndim
        return pl.BlockSpec(a.shape, lambda i: (0,) * nd, pipeline_mode=pl.Buffered(1))

    acts = [x2d, o_mla, o_moba, ga, gb]
    act_specs = [row(D_MODEL), row(MLA_W), row(MOBA_W), row(D_MODEL), row(D_MODEL)]
    scratch = [pltpu.VMEM((tm + HIST_ROWS, FF_CHUNK), F32)]
    if seq_tiles == 0:
        acts.append(hist)
        act_specs[1] = pl.BlockSpec((MLA_HEADS, tm, KV_LORA), lambda i: (0, i, 0))
        act_specs.append(row(2 * D_FF))
        wts = (wuv,) + tuple(wts)
        out_shape = (jax.ShapeDtypeStruct((n, D_MODEL), F32), jax.ShapeDtypeStruct((n, 2 * D_FF), F32))
        out_specs = (row(D_MODEL), row(2 * D_FF))
    else:
        nseq = n // (tm * seq_tiles)
        out_shape = (jax.ShapeDtypeStruct((n, D_MODEL), F32),
                     jax.ShapeDtypeStruct((nseq, CONV_W - 1, 2 * D_FF), F32))
        out_specs = (row(D_MODEL), pl.BlockSpec((None, CONV_W - 1, 2 * D_FF), lambda i: (i // seq_tiles, 0, 0)))
        scratch.append(pltpu.VMEM((HIST_ROWS, 2 * D_FF), F32))
    return pl.pallas_call(
        functools.partial(_tail_kernel, seq_tiles, seq_rows), grid=(n // tm,),
        in_specs=act_specs + [const(w) for w in wts], out_specs=out_specs, out_shape=out_shape,
        scratch_shapes=scratch, compiler_params=_cparams("arbitrary"), name="tail",
    )(*acts, *wts)


def _prep_tail_weights(w_br_mla, w_br_moba, w_out, g_ffn_norm, w_up, conv_w, conv_b, w_down, g_final):
    return (w_br_mla.astype(BF16), w_br_moba.astype(BF16), w_out.astype(BF16), g_ffn_norm.reshape(1, -1),
            w_up.astype(BF16), conv_w, conv_b.reshape(1, -1), w_down.astype(BF16), g_final.reshape(1, -1))


PAGE_CHUNK = 32
MLA_PAGE_CHUNK = 64


def _page_stream(pt_ref, hbm_bufs, sem, n_chunks, compute):
    d = pl.program_id(0)
    total = pl.num_programs(0) * n_chunks
    chunk = hbm_bufs[0][1].shape[1]

    def copies(g, slot):
        dd = g // n_chunks
        c = g % n_chunks
        out = []
        for k in range(chunk):
            page = pt_ref[dd, c * chunk + k]
            for idx, (hbm, buf) in enumerate(hbm_bufs):
                out.append(pltpu.make_async_copy(hbm.at[page], buf.at[slot, k], sem.at[slot, idx]))
        return out

    @pl.when(d == 0)
    def _():
        for cp in copies(0, 0):
            cp.start()

    def body(c, carry):
        g = d * n_chunks + c
        slot = g % 2

        @pl.when(g + 1 < total)
        def _():
            for cp in copies(g + 1, 1 - slot):
                cp.start()

        for cp in copies(g, slot):
            cp.wait()
        compute(slot, c)
        return carry

    lax.fori_loop(0, n_chunks, body, 0)


def _mla_sample_kernel(n_chunks, pt_ref, qa_ref, qr_ref, cn_ref, krn_ref, lat_hbm, kr_hbm, o_ref,
                       lat_buf, kr_buf, sem, m_ref, l_ref, acc_ref):
    qa = qa_ref[...]
    qr = qr_ref[...]
    rows = qa.shape[0]
    n_new = cn_ref.shape[0]
    m_ref[...] = jnp.full(m_ref.shape, NEG_INF, F32)
    l_ref[...] = jnp.zeros(l_ref.shape, F32)
    acc_ref[...] = jnp.zeros(acc_ref.shape, F32)

    def compute(slot, c):
        del c
        chunk = lat_buf.shape[1]
        cb = lat_buf[slot].reshape(chunk * PAGE_SIZE, KV_LORA).astype(BF16)
        kr = jnp.concatenate([kr_buf[slot, k] for k in range(chunk)], axis=1).astype(BF16)
        s = (_dot_nt(qa, cb) + _dot(qr, kr)) * MLA_SCALE
        m_old = m_ref[...]
        m_new = jnp.maximum(m_old, jnp.max(s, axis=1, keepdims=True))
        alpha = jnp.exp(m_old - m_new)
        p = jnp.exp(s - m_new)
        l_ref[...] = alpha * l_ref[...] + jnp.sum(p, axis=1, keepdims=True)
        acc_ref[...] = alpha * acc_ref[...] + _dot(p.astype(BF16), cb)
        m_ref[...] = m_new

    _page_stream(pt_ref, [(lat_hbm, lat_buf), (kr_hbm, kr_buf)], sem, n_chunks, compute)

    qaf = qa.astype(F32)
    qrf = qr.astype(F32)
    cn = cn_ref[...]
    krn = krn_ref[...]
    tq = lax.broadcasted_iota(jnp.int32, (rows, 1), 0) % n_new
    s_new = []
    for t in range(n_new):
        st = (jnp.sum(qaf * cn[t:t + 1], axis=1, keepdims=True)
              + jnp.sum(qrf * krn[t:t + 1], axis=1, keepdims=True)) * MLA_SCALE
        s_new.append(jnp.where(t <= tq, st, NEG_INF))
    m_old = m_ref[...]
    m_new = functools.reduce(jnp.maximum, s_new, m_old)
    alpha = jnp.exp(m_old - m_new)
    l = alpha * l_ref[...]
    acc = alpha * acc_ref[...]
    for t in range(n_new):
        p = jnp.exp(s_new[t] - m_new)
        l = l + p
        acc = acc + p * cn[t:t + 1]
    o_ref[...] = (acc / l).astype(o_ref.dtype)


def _mla_sample(page_table, qa_rows, qr_rows, c_new, kr_new, cache_lat, cache_kr_t):
    db, n_pages = page_table.shape
    chunk = min(MLA_PAGE_CHUNK, n_pages)
    n_chunks = n_pages // chunk
    rows = qa_rows.shape[1]
    t_new = c_new.shape[1]
    per_d = lambda *shape: pl.BlockSpec((None,) + shape, lambda d, pt: (d,) + (0,) * len(shape))
    grid_spec = pltpu.PrefetchScalarGridSpec(
        num_scalar_prefetch=1, grid=(db,),
        in_specs=[per_d(rows, KV_LORA), per_d(rows, ROPE_DIM), per_d(t_new, KV_LORA), per_d(t_new, ROPE_DIM),
                  pl.BlockSpec(memory_space=pl.ANY), pl.BlockSpec(memory_space=pl.ANY)],
        out_specs=per_d(rows, KV_LORA),
        scratch_shapes=[pltpu.VMEM((2, chunk, PAGE_SIZE, KV_LORA), F32),
                        pltpu.VMEM((2, chunk, ROPE_DIM, PAGE_SIZE), F32),
                        pltpu.SemaphoreType.DMA((2, 2)),
                        pltpu.VMEM((rows, 1), F32), pltpu.VMEM((rows, 1), F32), pltpu.VMEM((rows, KV_LORA), F32)])
    return pl.pallas_call(
        functools.partial(_mla_sample_kernel, n_chunks), grid_spec=grid_spec,
        out_shape=jax.ShapeDtypeStruct((db, rows, KV_LORA), BF16),
        compiler_params=_cparams("arbitrary"), name="mla_s",
    )(page_table, qa_rows, qr_rows, c_new, kr_new, cache_lat, cache_kr_t)


PAGES_PER_BLOCK = MOBA_BLOCK // PAGE_SIZE


def _moba_sample_scores_kernel(n_chunks, pt_ref, q_ref, kn_ref, k_hbm, psel_ref, idx_ref, pown_ref,
                               k_buf, sem, s_ref):
    q = q_ref[...]
    _page_stream(pt_ref, [(k_hbm, k_buf)], sem, n_chunks,
                 lambda slot, c: _moba_scores_chunk(q, k_buf, slot, c, s_ref))
    _moba_scores_finish(n_chunks * PAGE_CHUNK, q, kn_ref[...], s_ref, psel_ref, idx_ref, pown_ref)


def _moba_scores_chunk(q, k_buf, slot, c, s_ref):
    for k in range(PAGE_CHUNK):
        s_ref[c * PAGE_CHUNK + k] = _dot(q, k_buf[slot, k].astype(BF16))


def _moba_scores_finish(n_pages, q, kn, s_ref, psel_ref, idx_ref, pown_ref):
    rows = q.shape[0]
    n_new = kn.shape[0]
    n_blocks = n_pages // PAGES_PER_BLOCK
    lane = lax.broadcasted_iota(jnp.int32, (rows, LANES), 1)
    blk = lambda n: jnp.concatenate([s_ref[PAGES_PER_BLOCK * n + k] for k in range(PAGES_PER_BLOCK)], axis=1)
    bsum = [jnp.sum(blk(n), axis=1, keepdims=True) for n in range(n_blocks)]
    bs = jnp.full((rows, LANES), NEG_INF, F32)
    for n in range(n_blocks):
        bs = jnp.where(lane == n, bsum[n], bs)
    rank = []
    for n in range(n_blocks):
        beats = jnp.logical_or(bs > bsum[n], jnp.logical_and(bs == bsum[n], lane < n))
        rank.append(jnp.sum(beats.astype(F32), axis=1, keepdims=True))
    s_sel = [jnp.zeros((rows, MOBA_BLOCK), F32) for _ in range(MOBA_TOPK)]
    idx = jnp.zeros((rows, LANES), F32)
    for n in range(n_blocks):
        b = blk(n)
        for r in range(MOBA_TOPK):
            hit = rank[n] == float(r)
            s_sel[r] = jnp.where(hit, b, s_sel[r])
            idx = jnp.where(jnp.logical_and(hit, lane == r), float(n), idx)

    qf = q.astype(F32)
    tq = lax.broadcasted_iota(jnp.int32, (rows, 1), 0) % n_new
    s_new = [jnp.where(t <= tq, jnp.sum(qf * kn[t:t + 1], axis=1, keepdims=True), NEG_INF) for t in range(n_new)]

    m = functools.reduce(jnp.maximum, s_new + [jnp.max(s, axis=1, keepdims=True) for s in s_sel])
    p_new = [jnp.exp(s - m) for s in s_new]
    p_sel = [jnp.exp(s - m) for s in s_sel]
    l = functools.reduce(lambda a, b: a + b, p_new + [jnp.sum(p, axis=1, keepdims=True) for p in p_sel])
    inv = 1.0 / l
    for r in range(MOBA_TOPK):
        psel_ref[:, r * MOBA_BLOCK:(r + 1) * MOBA_BLOCK] = p_sel[r] * inv
    idx_ref[...] = idx
    pown = jnp.zeros((rows, LANES), F32)
    for t in range(n_new):
        pown = jnp.where(lane == t, p_new[t] * inv, pown)
    pown_ref[...] = pown


SEL_KEYS = MOBA_TOPK * MOBA_BLOCK
TAIL_PHASES = 4


def _tail_scores_kernel(seq_tiles, n_chunks, pt_ref, x_ref, om_ref, ob_ref, ga_ref, gb_ref, q_ref, kn_ref, k_hbm,
                        wa_ref, wb_ref, wo_ref, gf_ref, wup_ref, cw_ref, cb_ref, wdn_ref, gfin_ref,
                        y_ref, conv_ref, psel_ref, idx_ref, pown_ref, ue_ref, carry_ref, k_buf, sem, s_ref):
    i = pl.program_id(0)
    total = pl.num_programs(0) * TAIL_PHASES
    assert q_ref.shape[0] * n_chunks == TAIL_PHASES

    def copies(g, slot):
        dd = g // n_chunks
        c = g % n_chunks
        return [pltpu.make_async_copy(k_hbm.at[pt_ref[dd, c * PAGE_CHUNK + k]], k_buf.at[slot, k], sem.at[slot])
                for k in range(PAGE_CHUNK)]

    @pl.when(i == 0)
    def _():
        for cp in copies(0, 0):
            cp.start()

    def hook(ph):
        g = i * TAIL_PHASES + ph
        slot = ph % 2

        @pl.when(g + 1 < total)
        def _():
            for cp in copies(g + 1, 1 - slot):
                cp.start()

        for cp in copies(g, slot):
            cp.wait()
        j, c = divmod(ph, n_chunks)
        _moba_scores_chunk(q_ref[j], k_buf, slot, c, s_ref)
        if c == n_chunks - 1:
            _moba_scores_finish(n_chunks * PAGE_CHUNK, q_ref[j], kn_ref[j], s_ref,
                                psel_ref.at[j], idx_ref.at[j], pown_ref.at[j])

    _tail_kernel(seq_tiles, 0, x_ref, om_ref, ob_ref, ga_ref, gb_ref, wa_ref, wb_ref, wo_ref, gf_ref, wup_ref,
                 cw_ref, cb_ref, wdn_ref, gfin_ref, y_ref, conv_ref, ue_ref, carry_ref, hook=hook)


def _tail_with_scores(x2d, o_mla, o_moba, ga, gb, wts, seq_tiles, page_table, qbd, kn, kc):
    n = x2d.shape[0]
    tm = ROW_TILE
    steps = n // tm
    db, n_pages = page_table.shape
    n_chunks = n_pages // PAGE_CHUNK
    sp = db // steps
    rows = qbd.shape[1]
    t_new = kn.shape[1]
    assert db % steps == 0 and sp * n_chunks == TAIL_PHASES
    row = lambda w: pl.BlockSpec((tm, w), lambda i, pt: (i, 0))
    seq = lambda *shape: pl.BlockSpec((sp,) + shape, lambda i, pt: (i,) + (0,) * len(shape))

    def const(a):
        nd = a.ndim
        return pl.BlockSpec(a.shape, lambda i, pt: (0,) * nd, pipeline_mode=pl.Buffered(1))

    nseq = n // (tm * seq_tiles)
    grid_spec = pltpu.PrefetchScalarGridSpec(
        num_scalar_prefetch=1, grid=(steps,),
        in_specs=[row(D_MODEL), row(MLA_W), row(MOBA_W), row(D_MODEL), row(D_MODEL),
                  seq(rows, MOBA_W), seq(t_new, MOBA_W), pl.BlockSpec(memory_space=pl.ANY)]
                 + [const(w) for w in wts],
        out_specs=(row(D_MODEL),
                   pl.BlockSpec((None, CONV_W - 1, 2 * D_FF), lambda i, pt: (i // seq_tiles, 0, 0)),
                   seq(rows, SEL_KEYS), seq(rows, LANES), seq(rows, LANES)),
        scratch_shapes=[pltpu.VMEM((tm + HIST_ROWS, FF_CHUNK), F32), pltpu.VMEM((HIST_ROWS, 2 * D_FF), F32),
                        pltpu.VMEM((2, PAGE_CHUNK, MOBA_W, PAGE_SIZE), F32), pltpu.SemaphoreType.DMA((2,)),
                        pltpu.VMEM((n_pages, rows, PAGE_SIZE), F32)])
    return pl.pallas_call(
        functools.partial(_tail_scores_kernel, seq_tiles, n_chunks), grid_spec=grid_spec,
        out_shape=(jax.ShapeDtypeStruct((n, D_MODEL), F32),
                   jax.ShapeDtypeStruct((nseq, CONV_W - 1, 2 * D_FF), F32),
                   jax.ShapeDtypeStruct((db, rows, SEL_KEYS), F32),
                   jax.ShapeDtypeStruct((db, rows, LANES), F32),
                   jax.ShapeDtypeStruct((db, rows, LANES), F32)),
        compiler_params=_cparams("arbitrary"), name="tail_scores",
    )(page_table, x2d, o_mla, o_moba, ga, gb, qbd, kn, kc, *wts)


def _moba_sample_pv_kernel(n_new, pt_ref, sel_ref, psel_ref, pown_ref, vn_ref, v_hbm, o_ref, v_buf, sem):
    d = pl.program_id(0)
    nd = pl.num_programs(0)
    rows = MOBA_HEADS * n_new

    def copies(dd, slot):
        out = []
        for h in range(MOBA_HEADS):
            for t in range(n_new):
                for r in range(MOBA_TOPK):
                    blk = sel_ref[dd, (h * n_new + t) * MOBA_TOPK + r]
                    for k in range(PAGES_PER_BLOCK):
                        page = pt_ref[dd, blk * PAGES_PER_BLOCK + k]
                        piece = (t * MOBA_TOPK + r) * PAGES_PER_BLOCK + k
                        out.append(pltpu.make_async_copy(v_hbm.at[page, h], v_buf.at[slot, h, piece], sem.at[slot]))
        return out

    @pl.when(d == 0)
    def _():
        for cp in copies(0, 0):
            cp.start()

    slot = d % 2

    @pl.when(d + 1 < nd)
    def _():
        for cp in copies(d + 1, 1 - slot):
            cp.start()

    for cp in copies(d, slot):
        cp.wait()

    width = n_new * SEL_KEYS
    p = psel_ref[...]
    seg = lax.broadcasted_iota(jnp.int32, (rows, width), 1) // SEL_KEYS
    tok = lax.broadcasted_iota(jnp.int32, (rows, width), 0) % n_new
    p_all = jnp.where(seg == tok, jnp.concatenate([p] * n_new, axis=1), 0.0).astype(BF16)
    pown = pown_ref[...]
    row_head = lax.broadcasted_iota(jnp.int32, (rows, MOBA_HD), 0) // n_new
    acc = jnp.zeros((rows, MOBA_HD), F32)
    for h in range(MOBA_HEADS):
        v_h = jnp.concatenate([v_buf[slot, h, i] for i in range(v_buf.shape[2])], axis=1).astype(BF16)
        o_h = _dot_nt(p_all, v_h)
        vn = vn_ref[h]
        for t in range(n_new):
            o_h = o_h + pown[:, t:t + 1] * vn[t:t + 1]
        acc = jnp.where(row_head == h, o_h, acc)
    o_ref[...] = acc.astype(o_ref.dtype)


def _moba_sample_prep(q, k_new, cache_k_t):
    db, t_new = q.shape[:2]
    rows = t_new * MOBA_HEADS
    head_mask = (jnp.arange(MOBA_HEADS)[:, None] == jnp.arange(MOBA_HEADS)[None, :])
    qht = jnp.transpose(q, (0, 2, 1, 3))
    qbd = jnp.where(head_mask[None, :, None, :, None], qht[:, :, :, None, :], 0.0)
    qbd = qbd.reshape(db, rows, MOBA_W).astype(BF16)
    return qbd, k_new.reshape(db, t_new, MOBA_W), cache_k_t.reshape(cache_k_t.shape[0], MOBA_W, PAGE_SIZE)


def _per_seq(*shape):
    return pl.BlockSpec((None,) + shape, lambda d, *_: (d,) + (0,) * len(shape))


def _moba_sample_scores(page_table, qbd, kn, kc):
    db, n_pages = page_table.shape
    n_chunks = n_pages // PAGE_CHUNK
    rows, t_new = qbd.shape[1], kn.shape[1]
    per_d = _per_seq
    return pl.pallas_call(
        functools.partial(_moba_sample_scores_kernel, n_chunks),
        grid_spec=pltpu.PrefetchScalarGridSpec(
            num_scalar_prefetch=1, grid=(db,),
            in_specs=[per_d(rows, MOBA_W), per_d(t_new, MOBA_W), pl.BlockSpec(memory_space=pl.ANY)],
            out_specs=(per_d(rows, SEL_KEYS), per_d(rows, LANES), per_d(rows, LANES)),
            scratch_shapes=[pltpu.VMEM((2, PAGE_CHUNK, MOBA_W, PAGE_SIZE), F32), pltpu.SemaphoreType.DMA((2, 1)),
                            pltpu.VMEM((n_pages, rows, PAGE_SIZE), F32)]),
        out_shape=(jax.ShapeDtypeStruct((db, rows, SEL_KEYS), F32),
                   jax.ShapeDtypeStruct((db, rows, LANES), F32),
                   jax.ShapeDtypeStruct((db, rows, LANES), F32)),
        compiler_params=_cparams("arbitrary"), name="moba_s_scores",
    )(page_table, qbd, kn, kc)


def _moba_sample_pv(page_table, psel, idx, pown, v_new, cache_v_t):
    db, rows = psel.shape[:2]
    t_new = v_new.shape[1]
    per_d = _per_seq
    vn = jnp.transpose(v_new, (0, 2, 1, 3))
    sel = idx[:, :, :MOBA_TOPK].astype(jnp.int32).reshape(db, rows * MOBA_TOPK)
    o = pl.pallas_call(
        functools.partial(_moba_sample_pv_kernel, t_new),
        grid_spec=pltpu.PrefetchScalarGridSpec(
            num_scalar_prefetch=2, grid=(db,),
            in_specs=[per_d(rows, SEL_KEYS), per_d(rows, LANES), per_d(MOBA_HEADS, t_new, MOBA_HD),
                      pl.BlockSpec(memory_space=pl.ANY)],
            out_specs=per_d(rows, MOBA_HD),
            scratch_shapes=[pltpu.VMEM((2, MOBA_HEADS, t_new * SEL_KEYS // PAGE_SIZE, MOBA_HD, PAGE_SIZE), F32),
                            pltpu.SemaphoreType.DMA((2,))]),
        out_shape=jax.ShapeDtypeStruct((db, rows, MOBA_HD), BF16),
        compiler_params=_cparams("arbitrary"), name="moba_s_pv",
    )(page_table, sel, psel, pown, vn, cache_v_t)
    return jnp.transpose(o.reshape(db, MOBA_HEADS, t_new, MOBA_HD), (0, 2, 1, 3)).reshape(db, t_new, MOBA_W)


def kernel(x_prompt, x_sample, cache_mla_latent, cache_mla_krope, cache_moba_k, cache_moba_v,
           state_ffn_conv, page_table, g_attn_norm, w_in, g_qnorm, w_uq, g_kvnorm, w_uk, w_uv,
           w_br_mla, w_br_moba, w_out, g_ffn_norm, w_up, conv_w, conv_b, w_down, g_final):
    b, s, d_model = x_prompt.shape
    db, t_new, _ = x_sample.shape
    assert d_model == D_MODEL and s % ROW_TILE == 0 and (db * t_new) % ROW_TILE == 0
    assert page_table.shape == (db, PAST_LEN // PAGE_SIZE) and ROW_TILE == ATT_TILE == MOBA_BLOCK
    assert PAST_LEN % MOBA_BLOCK == 0 and ROW_TILE % t_new == 0

    pw = _prep_proj_weights(g_attn_norm, w_in, g_qnorm, w_uq, g_kvnorm, w_uk)
    tw = _prep_tail_weights(w_br_mla, w_br_moba, w_out, g_ffn_norm, w_up, conv_w, conv_b, w_down, g_final)
    wuv = _prep_wuv(w_uv)

    xp = x_prompt.reshape(b * s, D_MODEL)
    tabs_p = _rope_tables(jnp.arange(s, dtype=jnp.int32))
    ckv, kcat, ckvt, krt, kt, vt, qm, qcat, ga, gb = _proj(xp, tabs_p, s // ROW_TILE, pw)
    o_mla = _mla_prompt(qcat, kcat, ckvt, _prep_wuvt(w_uv), b, s)
    o_moba = _moba_prompt(qm, kt, vt, b, s)
    c_p = ckv.reshape(b, s, KV_LORA)
    kr_p = jnp.transpose(krt, (0, 2, 1))
    k_p = jnp.transpose(kt.reshape(b, MOBA_HEADS, MOBA_HD, s), (0, 3, 1, 2))
    v_p = jnp.transpose(vt.reshape(b, MOBA_HEADS, MOBA_HD, s), (0, 3, 1, 2))

    n_s = db * t_new
    xs = x_sample.reshape(n_s, D_MODEL)
    pos_s = jnp.tile(PAST_LEN + jnp.arange(t_new, dtype=jnp.int32), db)
    tabs_s = _rope_tables(pos_s)
    ckv_s, _, _, krt_s, kt_s, vt_s, qm_s, qcat_s, ga_s, gb_s = _proj(xs, tabs_s, n_s // ROW_TILE, pw)
    c_s = ckv_s.reshape(db, t_new, KV_LORA)
    kr_s = krt_s[0].T.reshape(db, t_new, ROPE_DIM)
    k_s = kt_s[0].T.reshape(db, t_new, MOBA_HEADS, MOBA_HD)
    v_s = vt_s[0].T.reshape(db, t_new, MOBA_HEADS, MOBA_HD)
    q_rows = jnp.transpose(qcat_s[0].reshape(MLA_HEADS, db, t_new, MLA_KW), (1, 0, 2, 3))
    q_rows = q_rows.reshape(db, MLA_HEADS * t_new, MLA_KW)
    qa_rows = q_rows[..., :KV_LORA]
    qr_rows = q_rows[..., KV_LORA:KV_LORA + ROPE_DIM]
    o_lat_s = _mla_sample(page_table, qa_rows, qr_rows, c_s, kr_s, cache_mla_latent,
                          jnp.transpose(cache_mla_krope, (0, 2, 1)))
    o_lat_s = jnp.transpose(o_lat_s.reshape(db, MLA_HEADS, t_new, KV_LORA), (1, 0, 2, 3))
    o_lat_s = o_lat_s.reshape(MLA_HEADS, n_s, KV_LORA)
    qbd, kn, kc = _moba_sample_prep(qm_s.reshape(db, t_new, MOBA_HEADS, MOBA_HD), k_s,
                                    jnp.transpose(cache_moba_k, (0, 2, 3, 1)))
    steps = (b * s) // ROW_TILE
    if db % steps == 0 and (db // steps) * (page_table.shape[1] // PAGE_CHUNK) == TAIL_PHASES:
        y_p, conv_p, psel, idx, pown = _tail_with_scores(xp, o_mla, o_moba, ga, gb, tw, s // ROW_TILE,
                                                         page_table, qbd, kn, kc)
    else:
        y_p, conv_p = _tail(xp, o_mla, o_moba, ga, gb, tw, s // ROW_TILE)
        psel, idx, pown = _moba_sample_scores(page_table, qbd, kn, kc)
    o_moba_s = _moba_sample_pv(page_table, psel, idx, pown, v_s, jnp.transpose(cache_moba_v, (0, 2, 3, 1)))
    hist = jnp.concatenate([state_ffn_conv, jnp.zeros((db, t_new - (CONV_W - 1), 2 * D_FF), F32)], axis=1)
    y_s, u_s = _tail(xs, o_lat_s, o_moba_s.reshape(n_s, MOBA_W), ga_s, gb_s, tw, 0,
                     hist=hist.reshape(n_s, -1), seq_rows=t_new, wuv=wuv, tm=SAMPLE_TILE)
    conv_s = u_s.reshape(db, t_new, 2 * D_FF)[:, t_new - (CONV_W - 1):]

    return (y_p.reshape(b, s, D_MODEL), y_s.reshape(db, t_new, D_MODEL), c_p, kr_p, k_p, v_p, conv_p,
            c_s, kr_s, k_s, v_s, conv_s)
```

```python
import functools

import jax
import jax.numpy as jnp
import numpy as np
from jax import lax
from jax.experimental import pallas as pl
from jax.experimental.pallas import tpu as pltpu

D_MODEL = 1024
PAST_LEN = 8192
PAGE_SIZE = 128

MLA_HEADS = 8
Q_LORA = 512
KV_LORA = 256
NOPE_DIM = 64
ROPE_DIM = 32
V_DIM = 64
MLA_QK = NOPE_DIM + ROPE_DIM
MLA_W = MLA_HEADS * V_DIM
MLA_SCALE = MLA_QK ** -0.5
MLA_THETA = 10000.0

MOBA_HEADS = 8
MOBA_HD = 64
MOBA_W = MOBA_HEADS * MOBA_HD
MOBA_ROT = MOBA_HD // 4
MOBA_BLOCK = 256
MOBA_TOPK = 3
MOBA_SCALE = MOBA_HD ** -0.5
ROPE_THETA = 500000.0

D_FF = 2816
CONV_W = 3
EPS = 1e-6

LANES = 128
ROW_TILE = 256
VMEM_LIMIT = 56 * 1024 * 1024
NEG_INF = float("-inf")

BF16 = jnp.bfloat16
F32 = jnp.float32


def _cparams(*sem):
    return pltpu.CompilerParams(dimension_semantics=sem, vmem_limit_bytes=VMEM_LIMIT)


def _dot(a, b):
    return jnp.dot(a, b, preferred_element_type=F32)


def _dot_nt(a, b):
    return lax.dot_general(a, b, (((1,), (1,)), ((), ())), preferred_element_type=F32)


def _rms(x, g):
    return x * lax.rsqrt(jnp.mean(x * x, axis=-1, keepdims=True) + EPS) * g


def _angles(pos, d, theta):
    half = d // 2
    inv = 1.0 / (theta ** (jnp.arange(half, dtype=F32) * (2.0 / d)))
    ang = pos.astype(F32)[:, None] * inv[None, :]
    return jnp.cos(ang), jnp.sin(ang)


def _row_table(cos, sin, group, lanes):
    half = cos.shape[1]
    p = cos.shape[0]
    lane = np.arange(lanes) % group
    first = lane < half
    second = (lane >= half) & (lane < 2 * half)
    idx = np.where(first, lane, np.where(second, lane - half, 0))
    c = jnp.where((first | second)[None, :], cos[:, idx], 1.0)
    sa = jnp.where(first[None, :], -sin[:, idx], 0.0)
    sb = jnp.where(second[None, :], sin[:, idx], 0.0)
    del p
    return jnp.concatenate([c, sa, sb], axis=1).astype(F32)


def _rope_tables(pos):
    cm, sm = _angles(pos, MOBA_ROT, ROPE_THETA)
    cr, sr = _angles(pos, ROPE_DIM, MLA_THETA)
    t_qm = _row_table(cm, sm, MOBA_HD, LANES)
    t_qr = _row_table(cr, sr, ROPE_DIM, LANES)
    t_kt = jnp.concatenate([cm.T, sm.T], axis=0)
    t_rt = jnp.concatenate([cr.T, sr.T], axis=0)
    return t_qm, t_qr, t_kt, t_rt


def _rope_rows(x, tab, half):
    c, sa, sb = tab[:, :LANES], tab[:, LANES:2 * LANES], tab[:, 2 * LANES:]
    outs = []
    for k in range(x.shape[1] // LANES):
        xk = x[:, k * LANES:(k + 1) * LANES]
        up = pltpu.roll(xk, LANES - half, axis=1)
        dn = pltpu.roll(xk, half, axis=1)
        outs.append(xk * c + up * sa + dn * sb)
    return jnp.concatenate(outs, axis=1) if len(outs) > 1 else outs[0]


W_T_ROWS = 2 * MOBA_W + ROPE_DIM
MLA_KW = KV_LORA + LANES


def _proj_kernel(x_ref, g_ref, wrow_ref, wt_ref, gq_ref, wuq_ref, gkv_ref, wuk_ref,
                 tqm_ref, tqr_ref, tkt_ref, trt_ref,
                 ckv_ref, kcat_ref, ckvt_ref, krt_ref, kt_ref, vt_ref,
                 qm_ref, qcat_ref, ga_ref, gb_ref):
    x = x_ref[...]
    h = _rms(x, g_ref[...]).astype(BF16)

    o = 0
    q_lat = _dot(h, wrow_ref[:, o:o + Q_LORA]); o += Q_LORA
    kv_lat = _dot(h, wrow_ref[:, o:o + KV_LORA]); o += KV_LORA
    q_m = _dot(h, wrow_ref[:, o:o + MOBA_W]); o += MOBA_W
    k_r = _dot(h, wrow_ref[:, o:o + LANES]); o += LANES
    ga_ref[...] = jax.nn.sigmoid(_dot(h, wrow_ref[:, o:o + D_MODEL])); o += D_MODEL
    gb_ref[...] = jax.nn.sigmoid(_dot(h, wrow_ref[:, o:o + D_MODEL]))

    ckv = _rms(kv_lat, gkv_ref[...])
    ckv_ref[...] = ckv
    kcat_ref[:, :KV_LORA] = ckv.astype(BF16)
    kcat_ref[:, KV_LORA:] = _rope_rows(k_r, tqr_ref[...], ROPE_DIM // 2).astype(BF16)
    ckvt_ref[...] = ckv.T.astype(BF16)

    qm_ref[...] = (_rope_rows(q_m, tqm_ref[...], MOBA_ROT // 2) * MOBA_SCALE).astype(BF16)

    qn = _rms(q_lat, gq_ref[...]).astype(BF16)
    q = _dot(qn, wuq_ref[...])
    q_nope = q[:, :MLA_HEADS * NOPE_DIM].astype(BF16)
    q_rope = _rope_rows(q[:, MLA_HEADS * NOPE_DIM:], tqr_ref[...], ROPE_DIM // 2)
    lane = lax.broadcasted_iota(jnp.int32, (q_rope.shape[0], LANES), 1)
    heads_per_tile = LANES // ROPE_DIM
    for j in range(MLA_HEADS // 2):
        qa2 = _dot(q_nope[:, j * LANES:(j + 1) * LANES], wuk_ref[j])
        qcat_ref[2 * j, :, :KV_LORA] = qa2[:, :KV_LORA].astype(BF16)
        qcat_ref[2 * j + 1, :, :KV_LORA] = qa2[:, KV_LORA:].astype(BF16)
    for hd in range(MLA_HEADS):
        chunk = q_rope[:, (hd // heads_per_tile) * LANES:(hd // heads_per_tile + 1) * LANES]
        shift = (hd % heads_per_tile) * ROPE_DIM
        if shift:
            chunk = pltpu.roll(chunk, LANES - shift, axis=1)
        qcat_ref[hd, :, KV_LORA:] = jnp.where(lane < ROPE_DIM, chunk, 0.0).astype(BF16)

    yt = _dot_nt(wt_ref[...], h)
    tkt = tkt_ref[...]
    cm, sm = tkt[:MOBA_ROT // 2], tkt[MOBA_ROT // 2:]
    hr = MOBA_ROT // 2
    for hd in range(MOBA_HEADS):
        r0 = hd * MOBA_HD
        x1 = yt[r0:r0 + hr]
        x2 = yt[r0 + hr:r0 + 2 * hr]
        kt_ref[r0:r0 + hr, :] = x1 * cm - x2 * sm
        kt_ref[r0 + hr:r0 + 2 * hr, :] = x2 * cm + x1 * sm
        kt_ref[r0 + 2 * hr:r0 + MOBA_HD, :] = yt[r0 + 2 * hr:r0 + MOBA_HD]
    vt_ref[...] = yt[MOBA_W:2 * MOBA_W]
    trt = trt_ref[...]
    cr, sr = trt[:ROPE_DIM // 2], trt[ROPE_DIM // 2:]
    y1 = yt[2 * MOBA_W:2 * MOBA_W + ROPE_DIM // 2]
    y2 = yt[2 * MOBA_W + ROPE_DIM // 2:]
    krt_ref[...] = jnp.concatenate([y1 * cr - y2 * sr, y2 * cr + y1 * sr], axis=0)


def _proj(x2d, tables, tiles_per_seq, wts):
    n = x2d.shape[0]
    tm = ROW_TILE
    nseq = n // (tm * tiles_per_seq)
    seq_len = tm * tiles_per_seq
    t_qm, t_qr, t_kt, t_rt = tables
    g_attn, w_row, w_t, g_q, w_uq, g_kv, w_ukp = wts

    def full(a):
        nd = a.ndim
        return pl.BlockSpec(a.shape, lambda i: (0,) * nd)

    row = lambda w: pl.BlockSpec((tm, w), lambda i: (i, 0))
    tab_row = lambda w: pl.BlockSpec((tm, w), lambda i: (i % tiles_per_seq, 0))
    tab_col = lambda r: pl.BlockSpec((r, tm), lambda i: (0, i % tiles_per_seq))
    col = lambda r: pl.BlockSpec((None, r, tm), lambda i: (i // tiles_per_seq, 0, i % tiles_per_seq))

    out_shape = (
        jax.ShapeDtypeStruct((n, KV_LORA), F32),
        jax.ShapeDtypeStruct((n, MLA_KW), BF16),
        jax.ShapeDtypeStruct((nseq, tiles_per_seq, KV_LORA, tm), BF16),
        jax.ShapeDtypeStruct((nseq, ROPE_DIM, seq_len), F32),
        jax.ShapeDtypeStruct((nseq, MOBA_W, seq_len), F32),
        jax.ShapeDtypeStruct((nseq, MOBA_W, seq_len), F32),
        jax.ShapeDtypeStruct((n, MOBA_W), BF16),
        jax.ShapeDtypeStruct((nseq, MLA_HEADS, seq_len, MLA_KW), BF16),
        jax.ShapeDtypeStruct((n, D_MODEL), F32),
        jax.ShapeDtypeStruct((n, D_MODEL), F32),
    )
    out_specs = (
        row(KV_LORA), row(MLA_KW),
        pl.BlockSpec((None, None, KV_LORA, tm), lambda i: (i // tiles_per_seq, i % tiles_per_seq, 0, 0)),
        col(ROPE_DIM), col(MOBA_W), col(MOBA_W),
        row(MOBA_W),
        pl.BlockSpec((None, MLA_HEADS, tm, MLA_KW), lambda i: (i // tiles_per_seq, 0, i % tiles_per_seq, 0)),
        row(D_MODEL), row(D_MODEL),
    )
    in_specs = [row(D_MODEL), full(g_attn), full(w_row), full(w_t), full(g_q), full(w_uq), full(g_kv),
                full(w_ukp), tab_row(3 * LANES), tab_row(3 * LANES), tab_col(MOBA_ROT), tab_col(ROPE_DIM)]
    return pl.pallas_call(
        _proj_kernel, grid=(n // tm,), in_specs=in_specs, out_specs=out_specs, out_shape=out_shape,
        compiler_params=_cparams("arbitrary"), name="proj",
    )(x2d, g_attn, w_row, w_t, g_q, w_uq, g_kv, w_ukp, t_qm, t_qr, t_kt, t_rt)


def _prep_proj_weights(g_attn_norm, w_in, g_qnorm, w_uq, g_kvnorm, w_uk):
    o_q, o_kv, o_kr = 0, Q_LORA, Q_LORA + KV_LORA
    o_qm = o_kr + ROPE_DIM
    o_km, o_vm = o_qm + MOBA_W, o_qm + 2 * MOBA_W
    o_ga = o_vm + MOBA_W
    w_kr = jnp.pad(w_in[:, o_kr:o_qm], ((0, 0), (0, LANES - ROPE_DIM)))
    w_row = jnp.concatenate([w_in[:, o_q:o_kr], w_in[:, o_qm:o_km], w_kr, w_in[:, o_ga:]], axis=1).astype(BF16)
    w_t = jnp.concatenate([w_in[:, o_km:o_vm], w_in[:, o_vm:o_ga], w_in[:, o_kr:o_qm]], axis=1).T.astype(BF16)
    wq = w_uq.reshape(Q_LORA, MLA_HEADS, MLA_QK)
    w_uq_p = jnp.concatenate([wq[:, :, :NOPE_DIM].reshape(Q_LORA, -1),
                              wq[:, :, NOPE_DIM:].reshape(Q_LORA, -1)], axis=1).astype(BF16)
    wk = jnp.transpose(w_uk, (1, 2, 0)).astype(BF16)
    z = jnp.zeros((NOPE_DIM, KV_LORA), BF16)
    w_ukp = jnp.stack([
        jnp.concatenate([jnp.concatenate([wk[2 * j], z], axis=1),
                         jnp.concatenate([z, wk[2 * j + 1]], axis=1)], axis=0)
        for j in range(MLA_HEADS // 2)])
    return (g_attn_norm.reshape(1, -1), w_row, w_t, g_qnorm.reshape(1, -1), w_uq_p,
            g_kvnorm.reshape(1, -1), w_ukp)


ATT_TILE = 256
MLA_KEY_TILES = 2


def _mla_prompt_kernel(q_ref, kcat_ref, ckvt_ref, wuvt_ref, o_ref, m_ref, l_ref, acc_ref):
    i = pl.program_id(1)
    t = ATT_TILE
    cols = MLA_HEADS * t
    q = q_ref[...].reshape(cols, MLA_KW)
    m_ref[...] = jnp.full(m_ref.shape, NEG_INF, F32)
    l_ref[...] = jnp.zeros(l_ref.shape, F32)
    acc_ref[...] = jnp.zeros(acc_ref.shape, F32)

    def step(j, mask, tiles=1):
        kc = kcat_ref[pl.ds(pl.multiple_of(j * t, t), tiles * t), :]
        st = _dot_nt(kc, q) * MLA_SCALE
        if mask is not None:
            st = jnp.where(mask, st, NEG_INF)
        m_old = m_ref[...]
        m_new = jnp.maximum(m_old, jnp.max(st, axis=0, keepdims=True))
        alpha = jnp.exp(m_old - m_new)
        p = jnp.exp(st - m_new)
        l_ref[...] = alpha * l_ref[...] + jnp.sum(p, axis=0, keepdims=True)
        p = p.astype(BF16)
        pv = sum(_dot(ckvt_ref[j + k], p[k * t:(k + 1) * t]) for k in range(tiles))
        acc_ref[...] = alpha * acc_ref[...] + pv
        m_ref[...] = m_new

    def body(jj, carry):
        step(jj * MLA_KEY_TILES, None, MLA_KEY_TILES)
        return carry

    lax.fori_loop(0, i // MLA_KEY_TILES, body, 0)
    for r in range(1, MLA_KEY_TILES):
        @pl.when(i % MLA_KEY_TILES >= r)
        def _(r=r):
            step((i // MLA_KEY_TILES) * MLA_KEY_TILES + r - 1, None)
    kpos = lax.broadcasted_iota(jnp.int32, (t, cols), 0)
    qpos = lax.broadcasted_iota(jnp.int32, (t, cols), 1) % t
    step(i, kpos <= qpos)

    o_lat = (acc_ref[...] / l_ref[...]).astype(BF16)
    o_t = jnp.concatenate([_dot(wuvt_ref[h], o_lat[:, h * t:(h + 1) * t]) for h in range(MLA_HEADS)], axis=0)
    o_ref[...] = o_t.T.astype(BF16)


def _prep_wuv(w_uv):
    w = jnp.transpose(w_uv, (1, 0, 2)).astype(BF16)
    z = jnp.zeros_like(w)
    even = jnp.concatenate([w, z], axis=2)
    odd = jnp.concatenate([z, w], axis=2)
    is_even = (jnp.arange(MLA_HEADS) % 2 == 0)[:, None, None]
    return jnp.where(is_even, even, odd)


def _prep_wuvt(w_uv):
    return jnp.transpose(w_uv, (1, 2, 0)).astype(BF16)


def _mla_prompt(qcat, kcat, ckvt, wuvt, b, s):
    t = ATT_TILE
    nq = s // t
    cols = MLA_HEADS * t
    return pl.pallas_call(
        _mla_prompt_kernel, grid=(b, nq),
        in_specs=[
            pl.BlockSpec((None, MLA_HEADS, t, MLA_KW), lambda bi, i: (bi, 0, i, 0)),
            pl.BlockSpec((s, MLA_KW), lambda bi, i: (bi, 0)),
            pl.BlockSpec((None, nq, KV_LORA, t), lambda bi, i: (bi, 0, 0, 0)),
            pl.BlockSpec(wuvt.shape, lambda bi, i: (0, 0, 0)),
        ],
        out_specs=pl.BlockSpec((t, MLA_W), lambda bi, i: (bi * nq + i, 0)),
        out_shape=jax.ShapeDtypeStruct((b * s, MLA_W), BF16),
        scratch_shapes=[pltpu.VMEM((1, cols), F32), pltpu.VMEM((1, cols), F32), pltpu.VMEM((KV_LORA, cols), F32)],
        compiler_params=_cparams("arbitrary", "arbitrary"), name="mla_p",
    )(qcat, kcat, ckvt, wuvt)


def _moba_attend(nb, q, krow_ref, vbf_ref, o_ref):
    t = MOBA_BLOCK
    nk = nb * t
    lane = lax.broadcasted_iota(jnp.int32, (t, LANES), 1)
    causal = lax.broadcasted_iota(jnp.int32, (t, t), 0) <= lax.broadcasted_iota(jnp.int32, (t, t), 1)
    n_past = nb - 1
    halves = []
    for half in range(2):
        qh = jnp.where((lane >= MOBA_HD) == bool(half), q, jnp.zeros_like(q))
        st = _dot_nt(krow_ref[0:nk, :], qh)
        blks = [st[n * t:(n + 1) * t] for n in range(nb)]
        blks[-1] = jnp.where(causal, blks[-1], NEG_INF)
        bmax = [jnp.max(b, axis=0, keepdims=True) for b in blks]
        mx = bmax[-1]
        if n_past > MOBA_TOPK:
            bsum = [jnp.sum(b, axis=0, keepdims=True) for b in blks[:-1]]
            keep = []
            for n in range(n_past):
                rank = jnp.zeros((1, t), F32)
                for m in range(n_past):
                    if m != n:
                        beats = (bsum[m] >= bsum[n]) if m < n else (bsum[m] > bsum[n])
                        rank = rank + beats.astype(F32)
                keep.append(rank < MOBA_TOPK)
                mx = jnp.maximum(mx, jnp.where(keep[n], bmax[n], NEG_INF))
        else:
            keep = [None] * n_past
            for n in range(n_past):
                mx = jnp.maximum(mx, bmax[n])
        l = jnp.zeros((1, t), F32)
        ps = []
        for n in range(nb):
            p = jnp.exp(blks[n] - mx)
            if n < n_past and keep[n] is not None:
                p = jnp.where(keep[n], p, 0.0)
            l = l + jnp.sum(p, axis=0, keepdims=True)
            ps.append(p.astype(BF16))
        pt = jnp.concatenate(ps, axis=0) if nb > 1 else ps[0]
        halves.append(_dot(vbf_ref[:, 0:nk], pt) / l)
    sub = lax.broadcasted_iota(jnp.int32, (LANES, t), 0)
    o_ref[...] = jnp.where(sub < MOBA_HD, halves[0], halves[1]).T.astype(BF16)


def _moba_prompt_kernel(nblk, q_ref, kt_ref, vt_ref, o_ref, krow_ref, vbf_ref):
    i = pl.program_id(2)

    @pl.when(i == 0)
    def _():
        krow_ref[...] = kt_ref[...].T.astype(BF16)
        vbf_ref[...] = vt_ref[...].astype(BF16)

    q = q_ref[...]
    for nb in range(1, nblk + 1):
        pl.when(i == nb - 1)(functools.partial(_moba_attend, nb, q, krow_ref, vbf_ref, o_ref))


def _moba_prompt(qm, kt, vt, b, s):
    t = MOBA_BLOCK
    nq = s // t
    npair = MOBA_W // LANES
    return pl.pallas_call(
        functools.partial(_moba_prompt_kernel, nq), grid=(b, npair, nq),
        in_specs=[
            pl.BlockSpec((t, LANES), lambda bi, j, i: (bi * nq + i, j)),
            pl.BlockSpec((None, LANES, s), lambda bi, j, i: (bi, j, 0)),
            pl.BlockSpec((None, LANES, s), lambda bi, j, i: (bi, j, 0)),
        ],
        out_specs=pl.BlockSpec((t, LANES), lambda bi, j, i: (bi * nq + i, j)),
        out_shape=jax.ShapeDtypeStruct((b * s, MOBA_W), BF16),
        scratch_shapes=[pltpu.VMEM((s, LANES), BF16), pltpu.VMEM((LANES, s), BF16)],
        compiler_params=_cparams("arbitrary", "arbitrary", "arbitrary"), name="moba_p",
    )(qm, kt, vt)


FF_CHUNK = D_FF // 2
HIST_ROWS = 8
SAMPLE_TILE = 128


def _tail_kernel(seq_tiles, seq_rows, *refs, hook=None):
    per_row_hist = seq_tiles == 0
    hook = hook or (lambda k: None)
    if per_row_hist:
        (x_ref, olat_ref, ob_ref, ga_ref, gb_ref, hist_ref, wuv_ref, wa_ref, wb_ref, wo_ref, gf_ref,
         wup_ref, cw_ref, cb_ref, wdn_ref, gfin_ref, y_ref, u_ref, ue_ref) = refs
        om = jnp.concatenate(
            [_dot(olat_ref[2 * j], wuv_ref[2 * j]) + _dot(olat_ref[2 * j + 1], wuv_ref[2 * j + 1])
             for j in range(MLA_HEADS // 2)], axis=1).astype(BF16)
    else:
        (x_ref, om_ref, ob_ref, ga_ref, gb_ref, wa_ref, wb_ref, wo_ref, gf_ref, wup_ref,
         cw_ref, cb_ref, wdn_ref, gfin_ref, y_ref, conv_ref, ue_ref, carry_ref) = refs
        om = om_ref[...]
    i = pl.program_id(0)
    tm = x_ref.shape[0]
    cw = FF_CHUNK

    a = _dot(om, wa_ref[...])
    b = _dot(ob_ref[...], wb_ref[...])
    mg = (ga_ref[...] * a + gb_ref[...] * b).astype(BF16)
    x1 = x_ref[...] + _dot(mg, wo_ref[...])
    hn = _rms(x1, gf_ref[...]).astype(BF16)

    if per_row_hist:
        tpos = lax.broadcasted_iota(jnp.int32, (tm, cw), 0) % seq_rows
        ue_ref[0:HIST_ROWS, :] = jnp.zeros((HIST_ROWS, cw), F32)
    else:
        first = (i % seq_tiles) == 0

        @pl.when(i == 0)
        def _():
            carry_ref[...] = jnp.zeros(carry_ref.shape, F32)

    hook(0)
    acc = x1
    for c in range(D_FF // cw):
        halves = []
        for half in range(2):
            if (c, half) == (1, 1):
                hook(2)
            c0 = half * D_FF + c * cw
            u = _dot(hn, wup_ref[:, c0:c0 + cw])
            if not per_row_hist:
                ue_ref[0:HIST_ROWS, :] = jnp.where(first, 0.0, carry_ref[:, c0:c0 + cw])
            ue_ref[HIST_ROWS:, :] = u
            u1 = ue_ref[HIST_ROWS - 1:HIST_ROWS - 1 + tm, :]
            u2 = ue_ref[HIST_ROWS - 2:HIST_ROWS - 2 + tm, :]
            if per_row_hist:
                hh = hist_ref[:, c0:c0 + cw]
                u1 = jnp.where(tpos < 1, pltpu.roll(hh, tm - 1, axis=0), u1)
                u2 = jnp.where(tpos < 2, hh, u2)
                u_ref[:, c0:c0 + cw] = u
            else:
                carry_ref[:, c0:c0 + cw] = u[tm - HIST_ROWS:, :]
                conv_ref[:, c0:c0 + cw] = u[tm - (CONV_W - 1):, :]
            halves.append(cb_ref[:, c0:c0 + cw] + cw_ref[0:1, c0:c0 + cw] * u2
                          + cw_ref[1:2, c0:c0 + cw] * u1 + cw_ref[2:3, c0:c0 + cw] * u)
        act = (jax.nn.silu(halves[0]) * halves[1]).astype(BF16)
        if c == 0:
            hook(1)
        acc = acc + _dot(act, wdn_ref[c * cw:(c + 1) * cw, :])
    y_ref[...] = _rms(acc, gfin_ref[...])
    hook(3)


def _tail(x2d, o_mla, o_moba, ga, gb, wts, seq_tiles, hist=None, seq_rows=0, wuv=None, tm=ROW_TILE):
    n = x2d.shape[0]
    row = lambda w: pl.BlockSpec((tm, w), lambda i: (i, 0))

    def const(a):
        nd = a.ndim
        return pl.BlockSpec(a.shape, lambda i: (0,) * nd, pipeline_mode=pl.Buffered(1))

    acts = [x2d, o_mla, o_moba, ga, gb]
    act_specs = [row(D_MODEL), row(MLA_W), row(MOBA_W), row(D_MODEL), row(D_MODEL)]
    scratch = [pltpu.VMEM((tm + HIST_ROWS, FF_CHUNK), F32)]
    if seq_tiles == 0:
        acts.append(hist)
        act_specs[1] = pl.BlockSpec((MLA_HEADS, tm, KV_LORA), lambda i: (0, i, 0))
        act_specs.append(row(2 * D_FF))
        wts = (wuv,) + tuple(wts)
        out_shape = (jax.ShapeDtypeStruct((n, D_MODEL), F32), jax.ShapeDtypeStruct((n, 2 * D_FF), F32))
        out_specs = (row(D_MODEL), row(2 * D_FF))
    else:
        nseq = n // (tm * seq_tiles)
        out_shape = (jax.ShapeDtypeStruct((n, D_MODEL), F32),
                     jax.ShapeDtypeStruct((nseq, CONV_W - 1, 2 * D_FF), F32))
        out_specs = (row(D_MODEL), pl.BlockSpec((None, CONV_W - 1, 2 * D_FF), lambda i: (i // seq_tiles, 0, 0)))
        scratch.append(pltpu.VMEM((HIST_ROWS, 2 * D_FF), F32))
    return pl.pallas_call(
        functools.partial(_tail_kernel, seq_tiles, seq_rows), grid=(n // tm,),
        in_specs=act_specs + [const(w) for w in wts], out_specs=out_specs, out_shape=out_shape,
        scratch_shapes=scratch, compiler_params=_cparams("arbitrary"), name="tail",
    )(*acts, *wts)


def _prep_tail_weights(w_br_mla, w_br_moba, w_out, g_ffn_norm, w_up, conv_w, conv_b, w_down, g_final):
    return (w_br_mla.astype(BF16), w_br_moba.astype(BF16), w_out.astype(BF16), g_ffn_norm.reshape(1, -1),
            w_up.astype(BF16), conv_w, conv_b.reshape(1, -1), w_down.astype(BF16), g_final.reshape(1, -1))


PAGE_CHUNK = 32
MLA_PAGE_CHUNK = 64


def _page_stream(pt_ref, hbm_bufs, sem, n_chunks, compute):
    d = pl.program_id(0)
    total = pl.num_programs(0) * n_chunks
    chunk = hbm_bufs[0][1].shape[1]

    def copies(g, slot):
        dd = g // n_chunks
        c = g % n_chunks
        out = []
        for k in range(chunk):
            page = pt_ref[dd, c * chunk + k]
            for idx, (hbm, buf) in enumerate(hbm_bufs):
                out.append(pltpu.make_async_copy(hbm.at[page], buf.at[slot, k], sem.at[slot, idx]))
        return out

    @pl.when(d == 0)
    def _():
        for cp in copies(0, 0):
            cp.start()

    def body(c, carry):
        g = d * n_chunks + c
        slot = g % 2

        @pl.when(g + 1 < total)
        def _():
            for cp in copies(g + 1, 1 - slot):
                cp.start()

        for cp in copies(g, slot):
            cp.wait()
        compute(slot, c)
        return carry

    lax.fori_loop(0, n_chunks, body, 0)


def _mla_sample_kernel(n_chunks, pt_ref, qa_ref, qr_ref, cn_ref, krn_ref, lat_hbm, kr_hbm, o_ref,
                       lat_buf, kr_buf, sem, m_ref, l_ref, acc_ref):
    qa = qa_ref[...]
    qr = qr_ref[...]
    rows = qa.shape[0]
    n_new = cn_ref.shape[0]
    m_ref[...] = jnp.full(m_ref.shape, NEG_INF, F32)
    l_ref[...] = jnp.zeros(l_ref.shape, F32)
    acc_ref[...] = jnp.zeros(acc_ref.shape, F32)

    def compute(slot, c):
        del c
        chunk = lat_buf.shape[1]
        cb = lat_buf[slot].reshape(chunk * PAGE_SIZE, KV_LORA).astype(BF16)
        kr = jnp.concatenate([kr_buf[slot, k] for k in range(chunk)], axis=1).astype(BF16)
        s = (_dot_nt(qa, cb) + _dot(qr, kr)) * MLA_SCALE
        m_old = m_ref[...]
        m_new = jnp.maximum(m_old, jnp.max(s, axis=1, keepdims=True))
        alpha = jnp.exp(m_old - m_new)
        p = jnp.exp(s - m_new)
        l_ref[...] = alpha * l_ref[...] + jnp.sum(p, axis=1, keepdims=True)
        acc_ref[...] = alpha * acc_ref[...] + _dot(p.astype(BF16), cb)
        m_ref[...] = m_new

    _page_stream(pt_ref, [(lat_hbm, lat_buf), (kr_hbm, kr_buf)], sem, n_chunks, compute)

    qaf = qa.astype(F32)
    qrf = qr.astype(F32)
    cn = cn_ref[...]
    krn = krn_ref[...]
    tq = lax.broadcasted_iota(jnp.int32, (rows, 1), 0) % n_new
    s_new = []
    for t in range(n_new):
        st = (jnp.sum(qaf * cn[t:t + 1], axis=1, keepdims=True)
              + jnp.sum(qrf * krn[t:t + 1], axis=1, keepdims=True)) * MLA_SCALE
        s_new.append(jnp.where(t <= tq, st, NEG_INF))
    m_old = m_ref[...]
    m_new = functools.reduce(jnp.maximum, s_new, m_old)
    alpha = jnp.exp(m_old - m_new)
    l = alpha * l_ref[...]
    acc = alpha * acc_ref[...]
    for t in range(n_new):
        p = jnp.exp(s_new[t] - m_new)
        l = l + p
        acc = acc + p * cn[t:t + 1]
    o_ref[...] = (acc / l).astype(o_ref.dtype)


def _mla_sample(page_table, qa_rows, qr_rows, c_new, kr_new, cache_lat, cache_kr_t):
    db, n_pages = page_table.shape
    chunk = min(MLA_PAGE_CHUNK, n_pages)
    n_chunks = n_pages // chunk
    rows = qa_rows.shape[1]
    t_new = c_new.shape[1]
    per_d = lambda *shape: pl.BlockSpec((None,) + shape, lambda d, pt: (d,) + (0,) * len(shape))
    grid_spec = pltpu.PrefetchScalarGridSpec(
        num_scalar_prefetch=1, grid=(db,),
        in_specs=[per_d(rows, KV_LORA), per_d(rows, ROPE_DIM), per_d(t_new, KV_LORA), per_d(t_new, ROPE_DIM),
                  pl.BlockSpec(memory_space=pl.ANY), pl.BlockSpec(memory_space=pl.ANY)],
        out_specs=per_d(rows, KV_LORA),
        scratch_shapes=[pltpu.VMEM((2, chunk, PAGE_SIZE, KV_LORA), F32),
                        pltpu.VMEM((2, chunk, ROPE_DIM, PAGE_SIZE), F32),
                        pltpu.SemaphoreType.DMA((2, 2)),
                        pltpu.VMEM((rows, 1), F32), pltpu.VMEM((rows, 1), F32), pltpu.VMEM((rows, KV_LORA), F32)])
    return pl.pallas_call(
        functools.partial(_mla_sample_kernel, n_chunks), grid_spec=grid_spec,
        out_shape=jax.ShapeDtypeStruct((db, rows, KV_LORA), BF16),
        compiler_params=_cparams("arbitrary"), name="mla_s",
    )(page_table, qa_rows, qr_rows, c_new, kr_new, cache_lat, cache_kr_t)


PAGES_PER_BLOCK = MOBA_BLOCK // PAGE_SIZE


def _moba_sample_scores_kernel(n_chunks, pt_ref, q_ref, kn_ref, k_hbm, psel_ref, idx_ref, pown_ref,
                               k_buf, sem, s_ref):
    q = q_ref[...]
    _page_stream(pt_ref, [(k_hbm, k_buf)], sem, n_chunks,
                 lambda slot, c: _moba_scores_chunk(q, k_buf, slot, c, s_ref))
    _moba_scores_finish(n_chunks * PAGE_CHUNK, q, kn_ref[...], s_ref, psel_ref, idx_ref, pown_ref)


def _moba_scores_chunk(q, k_buf, slot, c, s_ref):
    for k in range(PAGE_CHUNK):
        s_ref[c * PAGE_CHUNK + k] = _dot(q, k_buf[slot, k].astype(BF16))


def _moba_scores_finish(n_pages, q, kn, s_ref, psel_ref, idx_ref, pown_ref):
    rows = q.shape[0]
    n_new = kn.shape[0]
    n_blocks = n_pages // PAGES_PER_BLOCK
    lane = lax.broadcasted_iota(jnp.int32, (rows, LANES), 1)
    blk = lambda n: jnp.concatenate([s_ref[PAGES_PER_BLOCK * n + k] for k in range(PAGES_PER_BLOCK)], axis=1)
    bsum = [jnp.sum(blk(n), axis=1, keepdims=True) for n in range(n_blocks)]
    bs = jnp.full((rows, LANES), NEG_INF, F32)
    for n in range(n_blocks):
        bs = jnp.where(lane == n, bsum[n], bs)
    rank = []
    for n in range(n_blocks):
        beats = jnp.logical_or(bs > bsum[n], jnp.logical_and(bs == bsum[n], lane < n))
        rank.append(jnp.sum(beats.astype(F32), axis=1, keepdims=True))
    s_sel = [jnp.zeros((rows, MOBA_BLOCK), F32) for _ in range(MOBA_TOPK)]
    idx = jnp.zeros((rows, LANES), F32)
    for n in range(n_blocks):
        b = blk(n)
        for r in range(MOBA_TOPK):
            hit = rank[n] == float(r)
            s_sel[r] = jnp.where(hit, b, s_sel[r])
            idx = jnp.where(jnp.logical_and(hit, lane == r), float(n), idx)

    qf = q.astype(F32)
    tq = lax.broadcasted_iota(jnp.int32, (rows, 1), 0) % n_new
    s_new = [jnp.where(t <= tq, jnp.sum(qf * kn[t:t + 1], axis=1, keepdims=True), NEG_INF) for t in range(n_new)]

    m = functools.reduce(jnp.maximum, s_new + [jnp.max(s, axis=1, keepdims=True) for s in s_sel])
    p_new = [jnp.exp(s - m) for s in s_new]
    p_sel = [jnp.exp(s - m) for s in s_sel]
    l = functools.reduce(lambda a, b: a + b, p_new + [jnp.sum(p, axis=1, keepdims=True) for p in p_sel])
    inv = 1.0 / l
    for r in range(MOBA_TOPK):
        psel_ref[:, r * MOBA_BLOCK:(r + 1) * MOBA_BLOCK] = p_sel[r] * inv
    idx_ref[...] = idx
    pown = jnp.zeros((rows, LANES), F32)
    for t in range(n_new):
        pown = jnp.where(lane == t, p_new[t] * inv, pown)
    pown_ref[...] = pown


SEL_KEYS = MOBA_TOPK * MOBA_BLOCK
TAIL_PHASES = 4


def _tail_scores_kernel(seq_tiles, n_chunks, pt_ref, x_ref, om_ref, ob_ref, ga_ref, gb_ref, q_ref, kn_ref, k_hbm,
                        wa_ref, wb_ref, wo_ref, gf_ref, wup_ref, cw_ref, cb_ref, wdn_ref, gfin_ref,
                        y_ref, conv_ref, psel_ref, idx_ref, pown_ref, ue_ref, carry_ref, k_buf, sem, s_ref):
    i = pl.program_id(0)
    total = pl.num_programs(0) * TAIL_PHASES
    assert q_ref.shape[0] * n_chunks == TAIL_PHASES

    def copies(g, slot):
        dd = g // n_chunks
        c = g % n_chunks
        return [pltpu.make_async_copy(k_hbm.at[pt_ref[dd, c * PAGE_CHUNK + k]], k_buf.at[slot, k], sem.at[slot])
                for k in range(PAGE_CHUNK)]

    @pl.when(i == 0)
    def _():
        for cp in copies(0, 0):
            cp.start()

    def hook(ph):
        g = i * TAIL_PHASES + ph
        slot = ph % 2

        @pl.when(g + 1 < total)
        def _():
            for cp in copies(g + 1, 1 - slot):
                cp.start()

        for cp in copies(g, slot):
            cp.wait()
        j, c = divmod(ph, n_chunks)
        _moba_scores_chunk(q_ref[j], k_buf, slot, c, s_ref)
        if c == n_chunks - 1:
            _moba_scores_finish(n_chunks * PAGE_CHUNK, q_ref[j], kn_ref[j], s_ref,
                                psel_ref.at[j], idx_ref.at[j], pown_ref.at[j])

    _tail_kernel(seq_tiles, 0, x_ref, om_ref, ob_ref, ga_ref, gb_ref, wa_ref, wb_ref, wo_ref, gf_ref, wup_ref,
                 cw_ref, cb_ref, wdn_ref, gfin_ref, y_ref, conv_ref, ue_ref, carry_ref, hook=hook)


def _tail_with_scores(x2d, o_mla, o_moba, ga, gb, wts, seq_tiles, page_table, qbd, kn, kc):
    n = x2d.shape[0]
    tm = ROW_TILE
    steps = n // tm
    db, n_pages = page_table.shape
    n_chunks = n_pages // PAGE_CHUNK
    sp = db // steps
    rows = qbd.shape[1]
    t_new = kn.shape[1]
    assert db % steps == 0 and sp * n_chunks == TAIL_PHASES
    row = lambda w: pl.BlockSpec((tm, w), lambda i, pt: (i, 0))
    seq = lambda *shape: pl.BlockSpec((sp,) + shape, lambda i, pt: (i,) + (0,) * len(shape))

    def const(a):
        nd = a.ndim
        return pl.BlockSpec(a.shape, lambda i, pt: (0,) * nd, pipeline_mode=pl.Buffered(1))

    nseq = n // (tm * seq_tiles)
    grid_spec = pltpu.PrefetchScalarGridSpec(
        num_scalar_prefetch=1, grid=(steps,),
        in_specs=[row(D_MODEL), row(MLA_W), row(MOBA_W), row(D_MODEL), row(D_MODEL),
                  seq(rows, MOBA_W), seq(t_new, MOBA_W), pl.BlockSpec(memory_space=pl.ANY)]
                 + [const(w) for w in wts],
        out_specs=(row(D_MODEL),
                   pl.BlockSpec((None, CONV_W - 1, 2 * D_FF), lambda i, pt: (i // seq_tiles, 0, 0)),
                   seq(rows, SEL_KEYS), seq(rows, LANES), seq(rows, LANES)),
        scratch_shapes=[pltpu.VMEM((tm + HIST_ROWS, FF_CHUNK), F32), pltpu.VMEM((HIST_ROWS, 2 * D_FF), F32),
                        pltpu.VMEM((2, PAGE_CHUNK, MOBA_W, PAGE_SIZE), F32), pltpu.SemaphoreType.DMA((2,)),
                        pltpu.VMEM((n_pages, rows, PAGE_SIZE), F32)])
    return pl.pallas_call(
        functools.partial(_tail_scores_kernel, seq_tiles, n_chunks), grid_spec=grid_spec,
        out_shape=(jax.ShapeDtypeStruct((n, D_MODEL), F32),
                   jax.ShapeDtypeStruct((nseq, CONV_W - 1, 2 * D_FF), F32),
                   jax.ShapeDtypeStruct((db, rows, SEL_KEYS), F32),
                   jax.ShapeDtypeStruct((db, rows, LANES), F32),
                   jax.ShapeDtypeStruct((db, rows, LANES), F32)),
        compiler_params=_cparams("arbitrary"), name="tail_scores",
    )(page_table, x2d, o_mla, o_moba, ga, gb, qbd, kn, kc, *wts)


def _moba_sample_pv_kernel(n_new, pt_ref, sel_ref, psel_ref, pown_ref, vn_ref, v_hbm, o_ref, v_buf, sem):
    d = pl.program_id(0)
    nd = pl.num_programs(0)
    rows = MOBA_HEADS * n_new

    def copies(dd, slot):
        out = []
        for h in range(MOBA_HEADS):
            for t in range(n_new):
                for r in range(MOBA_TOPK):
                    blk = sel_ref[dd, (h * n_new + t) * MOBA_TOPK + r]
                    for k in range(PAGES_PER_BLOCK):
                        page = pt_ref[dd, blk * PAGES_PER_BLOCK + k]
                        piece = (t * MOBA_TOPK + r) * PAGES_PER_BLOCK + k
                        out.append(pltpu.make_async_copy(v_hbm.at[page, h], v_buf.at[slot, h, piece], sem.at[slot]))
        return out

    @pl.when(d == 0)
    def _():
        for cp in copies(0, 0):
            cp.start()

    slot = d % 2

    @pl.when(d + 1 < nd)
    def _():
        for cp in copies(d + 1, 1 - slot):
            cp.start()

    for cp in copies(d, slot):
        cp.wait()

    width = n_new * SEL_KEYS
    p = psel_ref[...]
    seg = lax.broadcasted_iota(jnp.int32, (rows, width), 1) // SEL_KEYS
    tok = lax.broadcasted_iota(jnp.int32, (rows, width), 0) % n_new
    p_all = jnp.where(seg == tok, jnp.concatenate([p] * n_new, axis=1), 0.0).astype(BF16)
    pown = pown_ref[...]
    row_head = lax.broadcasted_iota(jnp.int32, (rows, MOBA_HD), 0) // n_new
    acc = jnp.zeros((rows, MOBA_HD), F32)
    for h in range(MOBA_HEADS):
        v_h = jnp.concatenate([v_buf[slot, h, i] for i in range(v_buf.shape[2])], axis=1).astype(BF16)
        o_h = _dot_nt(p_all, v_h)
        vn = vn_ref[h]
        for t in range(n_new):
            o_h = o_h + pown[:, t:t + 1] * vn[t:t + 1]
        acc = jnp.where(row_head == h, o_h, acc)
    o_ref[...] = acc.astype(o_ref.dtype)


def _moba_sample_prep(q, k_new, cache_k_t):
    db, t_new = q.shape[:2]
    rows = t_new * MOBA_HEADS
    head_mask = (jnp.arange(MOBA_HEADS)[:, None] == jnp.arange(MOBA_HEADS)[None, :])
    qht = jnp.transpose(q, (0, 2, 1, 3))
    qbd = jnp.where(head_mask[None, :, None, :, None], qht[:, :, :, None, :], 0.0)
    qbd = qbd.reshape(db, rows, MOBA_W).astype(BF16)
    return qbd, k_new.reshape(db, t_new, MOBA_W), cache_k_t.reshape(cache_k_t.shape[0], MOBA_W, PAGE_SIZE)


def _per_seq(*shape):
    return pl.BlockSpec((None,) + shape, lambda d, *_: (d,) + (0,) * len(shape))


def _moba_sample_scores(page_table, qbd, kn, kc):
    db, n_pages = page_table.shape
    n_chunks = n_pages // PAGE_CHUNK
    rows, t_new = qbd.shape[1], kn.shape[1]
    per_d = _per_seq
    return pl.pallas_call(
        functools.partial(_moba_sample_scores_kernel, n_chunks),
        grid_spec=pltpu.PrefetchScalarGridSpec(
            num_scalar_prefetch=1, grid=(db,),
            in_specs=[per_d(rows, MOBA_W), per_d(t_new, MOBA_W), pl.BlockSpec(memory_space=pl.ANY)],
            out_specs=(per_d(rows, SEL_KEYS), per_d(rows, LANES), per_d(rows, LANES)),
            scratch_shapes=[pltpu.VMEM((2, PAGE_CHUNK, MOBA_W, PAGE_SIZE), F32), pltpu.SemaphoreType.DMA((2, 1)),
                            pltpu.VMEM((n_pages, rows, PAGE_SIZE), F32)]),
        out_shape=(jax.ShapeDtypeStruct((db, rows, SEL_KEYS), F32),
                   jax.ShapeDtypeStruct((db, rows, LANES), F32),
                   jax.ShapeDtypeStruct((db, rows, LANES), F32)),
        compiler_params=_cparams("arbitrary"), name="moba_s_scores",
    )(page_table, qbd, kn, kc)


def _moba_sample_pv(page_table, psel, idx, pown, v_new, cache_v_t):
    db, rows = psel.shape[:2]
    t_new = v_new.shape[1]
    per_d = _per_seq
    vn = jnp.transpose(v_new, (0, 2, 1, 3))
    sel = idx[:, :, :MOBA_TOPK].astype(jnp.int32).reshape(db, rows * MOBA_TOPK)
    o = pl.pallas_call(
        functools.partial(_moba_sample_pv_kernel, t_new),
        grid_spec=pltpu.PrefetchScalarGridSpec(
            num_scalar_prefetch=2, grid=(db,),
            in_specs=[per_d(rows, SEL_KEYS), per_d(rows, LANES), per_d(MOBA_HEADS, t_new, MOBA_HD),
                      pl.BlockSpec(memory_space=pl.ANY)],
            out_specs=per_d(rows, MOBA_HD),
            scratch_shapes=[pltpu.VMEM((2, MOBA_HEADS, t_new * SEL_KEYS // PAGE_SIZE, MOBA_HD, PAGE_SIZE), F32),
                            pltpu.SemaphoreType.DMA((2,))]),
        out_shape=jax.ShapeDtypeStruct((db, rows, MOBA_HD), BF16),
        compiler_params=_cparams("arbitrary"), name="moba_s_pv",
    )(page_table, sel, psel, pown, vn, cache_v_t)
    return jnp.transpose(o.reshape(db, MOBA_HEADS, t_new, MOBA_HD), (0, 2, 1, 3)).reshape(db, t_new, MOBA_W)


def kernel(x_prompt, x_sample, cache_mla_latent, cache_mla_krope, cache_moba_k, cache_moba_v,
           state_ffn_conv, page_table, g_attn_norm, w_in, g_qnorm, w_uq, g_kvnorm, w_uk, w_uv,
           w_br_mla, w_br_moba, w_out, g_ffn_norm, w_up, conv_w, conv_b, w_down, g_final):
    b, s, d_model = x_prompt.shape
    db, t_new, _ = x_sample.shape
    assert d_model == D_MODEL and s % ROW_TILE == 0 and (db * t_new) % ROW_TILE == 0
    assert page_table.shape == (db, PAST_LEN // PAGE_SIZE) and ROW_TILE == ATT_TILE == MOBA_BLOCK
    assert PAST_LEN % MOBA_BLOCK == 0 and ROW_TILE % t_new == 0

    pw = _prep_proj_weights(g_attn_norm, w_in, g_qnorm, w_uq, g_kvnorm, w_uk)
    tw = _prep_tail_weights(w_br_mla, w_br_moba, w_out, g_ffn_norm, w_up, conv_w, conv_b, w_down, g_final)
    wuv = _prep_wuv(w_uv)

    xp = x_prompt.reshape(b * s, D_MODEL)
    tabs_p = _rope_tables(jnp.arange(s, dtype=jnp.int32))
    ckv, kcat, ckvt, krt, kt, vt, qm, qcat, ga, gb = _proj(xp, tabs_p, s // ROW_TILE, pw)
    o_mla = _mla_prompt(qcat, kcat, ckvt, _prep_wuvt(w_uv), b, s)
    o_moba = _moba_prompt(qm, kt, vt, b, s)
    c_p = ckv.reshape(b, s, KV_LORA)
    kr_p = jnp.transpose(krt, (0, 2, 1))
    k_p = jnp.transpose(kt.reshape(b, MOBA_HEADS, MOBA_HD, s), (0, 3, 1, 2))
    v_p = jnp.transpose(vt.reshape(b, MOBA_HEADS, MOBA_HD, s), (0, 3, 1, 2))

    n_s = db * t_new
    xs = x_sample.reshape(n_s, D_MODEL)
    pos_s = jnp.tile(PAST_LEN + jnp.arange(t_new, dtype=jnp.int32), db)
    tabs_s = _rope_tables(pos_s)
    ckv_s, _, _, krt_s, kt_s, vt_s, qm_s, qcat_s, ga_s, gb_s = _proj(xs, tabs_s, n_s // ROW_TILE, pw)
    c_s = ckv_s.reshape(db, t_new, KV_LORA)
    kr_s = krt_s[0].T.reshape(db, t_new, ROPE_DIM)
    k_s = kt_s[0].T.reshape(db, t_new, MOBA_HEADS, MOBA_HD)
    v_s = vt_s[0].T.reshape(db, t_new, MOBA_HEADS, MOBA_HD)
    q_rows = jnp.transpose(qcat_s[0].reshape(MLA_HEADS, db, t_new, MLA_KW), (1, 0, 2, 3))
    q_rows = q_rows.reshape(db, MLA_HEADS * t_new, MLA_KW)
    qa_rows = q_rows[..., :KV_LORA]
    qr_rows = q_rows[..., KV_LORA:KV_LORA + ROPE_DIM]
    o_lat_s = _mla_sample(page_table, qa_rows, qr_rows, c_s, kr_s, cache_mla_latent,
                          jnp.transpose(cache_mla_krope, (0, 2, 1)))
    o_lat_s = jnp.transpose(o_lat_s.reshape(db, MLA_HEADS, t_new, KV_LORA), (1, 0, 2, 3))
    o_lat_s = o_lat_s.reshape(MLA_HEADS, n_s, KV_LORA)
    qbd, kn, kc = _moba_sample_prep(qm_s.reshape(db, t_new, MOBA_HEADS, MOBA_HD), k_s,
                                    jnp.transpose(cache_moba_k, (0, 2, 3, 1)))
    steps = (b * s) // ROW_TILE
    if db % steps == 0 and (db // steps) * (page_table.shape[1] // PAGE_CHUNK) == TAIL_PHASES:
        y_p, conv_p, psel, idx, pown = _tail_with_scores(xp, o_mla, o_moba, ga, gb, tw, s // ROW_TILE,
                                                         page_table, qbd, kn, kc)
    else:
        y_p, conv_p = _tail(xp, o_mla, o_moba, ga, gb, tw, s // ROW_TILE)
        psel, idx, pown = _moba_sample_scores(page_table, qbd, kn, kc)
    o_moba_s = _moba_sample_pv(page_table, psel, idx, pown, v_s, jnp.transpose(cache_moba_v, (0, 2, 3, 1)))
    hist = jnp.concatenate([state_ffn_conv, jnp.zeros((db, t_new - (CONV_W - 1), 2 * D_FF), F32)], axis=1)
    y_s, u_s = _tail(xs, o_lat_s, o_moba_s.reshape(n_s, MOBA_W), ga_s, gb_s, tw, 0,
                     hist=hist.reshape(n_s, -1), seq_rows=t_new, wuv=wuv, tm=SAMPLE_TILE)
    conv_s = u_s.reshape(db, t_new, 2 * D_FF)[:, t_new - (CONV_W - 1):]

    return (y_p.reshape(b, s, D_MODEL), y_s.reshape(db, t_new, D_MODEL), c_p, kr_p, k_p, v_p, conv_p,
            c_s, kr_s, k_s, v_s, conv_s)
```
